```python
import math
import jax, jax.numpy as jnp
from jax import lax
import numpy as np

D_MODEL = 1024
BATCH = 32
SEQ = 2048
DEPTH = 2

N_EVEN = (DEPTH + 1) // 2
N_ODD = DEPTH // 2
CHUNK = 64
CONV_K = 4
LN_EPS = 1e-5
NORM_EPS = 1e-6
RET_H = 4
RET_DK = D_MODEL // 8
RET_DV = D_MODEL // 4
ROPE_BASE = 10000.0
ML_H = 4
ML_DK = D_MODEL // 8
ML_DV = D_MODEL // 4
GLA_H = 4
GLA_DK = D_MODEL // 8
GLA_DV = D_MODEL // 4
GLA_RANK = 16
GLA_TAU = 16.0
SSD_HEADDIM = 64
SSD_DINNER = D_MODEL
SSD_H = SSD_DINNER // SSD_HEADDIM
SSD_GROUPS = 4
SSD_HPG = SSD_H // SSD_GROUPS
SSD_N = 128
N_EXPERTS = 16
N_EXPERT_GROUPS = 4
EXPERTS_PER_GROUP = N_EXPERTS // N_EXPERT_GROUPS
TOP_K = 2
GROUP_SCORE_TOPK = 2
D_FF = D_MODEL // 2
DEEPNORM_ALPHA = (2.0 * DEPTH) ** 0.25
DEEPNORM_BETA = (8.0 * DEPTH) ** -0.25
RET_SPLITS = (RET_H * RET_DK, RET_H * RET_DK, RET_H * RET_DV, RET_H * RET_DV)
ML_SPLITS = (ML_H * ML_DK, ML_H * ML_DK, ML_H * ML_DV, ML_H * ML_DV, ML_H, ML_H)
P_AB = sum(RET_SPLITS) + sum(ML_SPLITS)
MIX_AB = RET_H * RET_DV + ML_H * ML_DV
GLA_SPLITS = (GLA_H * GLA_DK, GLA_H * GLA_DK, GLA_H * GLA_DV, GLA_H * GLA_DV, GLA_RANK)
SSD_SPLITS = (SSD_DINNER, SSD_DINNER + 2 * SSD_GROUPS * SSD_N, SSD_H)
P_CD = sum(GLA_SPLITS) + sum(SSD_SPLITS)
MIX_CD = GLA_H * GLA_DV + SSD_DINNER

kernel_name = 'hybrid_retention_mlstm_gla_ssd_grouped_moe_deepnorm'


def _split(t, sizes):
    return jnp.split(t, np.cumsum(sizes)[:-1].tolist(), axis=-1)


def _heads(t, h):
    b, s, hd = t.shape
    return t.reshape(b, s, h, hd // h).transpose(0, 2, 1, 3)


def _chunks(t):
    b, h, s = t.shape[:3]
    return jnp.moveaxis(t.reshape(b, h, s // CHUNK, CHUNK, *t.shape[3:]), 2, 0)


def _unchunk(t):
    nc, b, h, c, d = t.shape
    return t.transpose(1, 0, 3, 2, 4).reshape(b, nc * c, h * d)


def _seq_chunks(t):
    b, s = t.shape[:2]
    return jnp.moveaxis(t.reshape(b, s // CHUNK, CHUNK, *t.shape[2:]), 1, 0)


def _layernorm(x, g, b):
    xf = x.astype(jnp.float32)
    mu = xf.mean(-1, keepdims=True)
    var = jnp.square(xf - mu).mean(-1, keepdims=True)
    return ((xf - mu) * lax.rsqrt(var + LN_EPS) * g.astype(jnp.float32) + b.astype(jnp.float32)).astype(x.dtype)


def _head_norm(y, n_groups, g, center):
    b, s, w = y.shape
    yf = y.astype(jnp.float32).reshape(b, s, n_groups, w // n_groups)
    if center:
        yf = yf - yf.mean(-1, keepdims=True)
    yf = yf * lax.rsqrt(jnp.square(yf).mean(-1, keepdims=True) + NORM_EPS)
    return yf.reshape(b, s, w) * g.astype(jnp.float32)


def _causal_conv(u, w, bias):
    ch = u.shape[-1]
    out = lax.conv_general_dilated(u, w[:, None, :].astype(u.dtype), window_strides=(1,),
                                   padding=[(CONV_K - 1, 0)], dimension_numbers=('NWC', 'WIO', 'NWC'),
                                   feature_group_count=ch)
    return out + bias


def _rotary(t, pos):
    d = t.shape[-1]
    inv_freq = ROPE_BASE ** (-jnp.arange(0, d, 2, dtype=jnp.float32) / d)
    ang = pos.astype(jnp.float32)[:, None, :, None] * inv_freq
    cos, sin = jnp.cos(ang), jnp.sin(ang)
    t1, t2 = t[..., 0::2], t[..., 1::2]
    return jnp.stack([t1 * cos - t2 * sin, t1 * sin + t2 * cos], axis=-1).reshape(t.shape)


def _retention(q, k, v):
    b, h, s, dk = q.shape
    dv = v.shape[-1]
    log_g = jnp.log(1.0 - 2.0 ** (-5.0 - jnp.arange(h, dtype=jnp.float32)))
    idx = jnp.arange(CHUNK, dtype=jnp.float32)
    diff = idx[:, None] - idx[None, :]
    mask = diff >= 0
    decay_intra = jnp.where(mask, jnp.exp(log_g[:, None, None] * jnp.where(mask, diff, 0.0)), 0.0)
    q_decay = jnp.exp(log_g[:, None] * (idx + 1.0))[..., None]
    k_decay = jnp.exp(log_g[:, None] * (CHUNK - 1.0 - idx))[..., None]
    chunk_decay = jnp.exp(log_g * CHUNK)[:, None, None]

    def step(state, inp):
        qi, ki, vi = inp
        scores = jnp.einsum('bhtd,bhsd->bhts', qi, ki) * decay_intra
        y = jnp.einsum('bhts,bhsv->bhtv', scores, vi) + jnp.einsum('bhtd,bhdv->bhtv', qi * q_decay, state)
        state = chunk_decay * state + jnp.einsum('bhsd,bhsv->bhdv', ki * k_decay, vi)
        return state, y

    state0 = jnp.zeros((b, h, dk, dv), jnp.float32)
    _, y = lax.scan(step, state0, (_chunks(q), _chunks(k), _chunks(v)))
    return _unchunk(y)


def _mlstm(q, k, v, i_pre, logf):
    b, h, s, dk = q.shape
    dv = v.shape[-1]
    causal = jnp.tril(jnp.ones((CHUNK, CHUNK), bool))

    def step(carry, inp):
        c_st, n_st, m_st = carry
        qi, ki, vi, ii, fi = inp
        bcum = jnp.cumsum(fi, axis=-1)
        a = bcum + m_st[..., None]
        d = jnp.where(causal, bcum[..., :, None] - bcum[..., None, :] + ii[..., None, :], -jnp.inf)
        m_t = jnp.maximum(a, d.max(-1))
        w_inter = jnp.exp(a - m_t)
        qk = jnp.einsum('bhtd,bhsd->bhts', qi, ki) * jnp.exp(d - m_t[..., None])
        num = jnp.einsum('bhts,bhsv->bhtv', qk, vi) + w_inter[..., None] * jnp.einsum('bhtd,bhdv->bhtv', qi, c_st)
        den = qk.sum(-1) + w_inter * jnp.einsum('bhtd,bhd->bht', qi, n_st)
        h_t = num / jnp.maximum(jnp.abs(den), jnp.exp(-m_t))[..., None]
        b_tot = bcum[..., -1]
        log_ws = b_tot[..., None] - bcum + ii
        m_new = jnp.maximum(b_tot + m_st, log_ws.max(-1))
        ks = ki * jnp.exp(log_ws - m_new[..., None])[..., None]
        decay = jnp.exp(b_tot + m_st - m_new)
        c_new = decay[..., None, None] * c_st + jnp.einsum('bhsd,bhsv->bhdv', ks, vi)
        n_new = decay[..., None] * n_st + ks.sum(-2)
        return (c_new, n_new, m_new), h_t

    carry0 = (jnp.zeros((b, h, dk, dv), jnp.float32), jnp.zeros((b, h, dk), jnp.float32),
              jnp.zeros((b, h), jnp.float32))
    _, y = lax.scan(step, carry0, (_chunks(q), _chunks(k), _chunks(v), _chunks(i_pre), _chunks(logf)))
    return _unchunk(y)


def _gla(q, k, v, log_alpha):
    b, h, s, dk = q.shape
    dv = v.shape[-1]
    causal = jnp.tril(jnp.ones((CHUNK, CHUNK), bool))[:, :, None]

    def step(state, inp):
        qi, ki, vi, gi = inp
        bcum = jnp.cumsum(gi, axis=-2)
        rel = jnp.where(causal, bcum[..., :, None, :] - bcum[..., None, :, :], -jnp.inf)
        scores = jnp.einsum('bhtd,bhsd,bhtsd->bhts', qi, ki, jnp.exp(rel))
        y = jnp.einsum('bhts,bhsv->bhtv', scores, vi) + jnp.einsum('bhtd,bhdv->bhtv', qi * jnp.exp(bcum), state)
        b_tot = bcum[..., -1:, :]
        state = jnp.exp(b_tot)[..., 0, :, None] * state + jnp.einsum('bhsd,bhsv->bhdv', ki * jnp.exp(b_tot - bcum), vi)
        return state, y

    state0 = jnp.zeros((b, h, dk, dv), jnp.float32)
    _, y = lax.scan(step, state0, (_chunks(q), _chunks(k), _chunks(v), _chunks(log_alpha)))
    return _unchunk(y)


def _ssd(xdt, log_a, bm, cm):
    b, s, g, r, p = xdt.shape
    n = bm.shape[-1]
    causal = jnp.tril(jnp.ones((CHUNK, CHUNK), bool))[None, :, :, None, None]

    def step(state, inp):
        xi, lai, bi, ci = inp
        cum = jnp.cumsum(lai, axis=1)
        seg = jnp.where(causal, cum[:, :, None] - cum[:, None, :], -jnp.inf)
        w = jnp.einsum('btgn,bsgn->btsg', ci, bi)[..., None] * jnp.exp(seg)
        y = jnp.einsum('btsgr,bsgrp->btgrp', w, xi) + jnp.exp(cum)[..., None] * jnp.einsum('btgn,bgrnp->btgrp', ci, state)
        last = cum[:, -1]
        xs = xi * jnp.exp(last[:, None] - cum)[..., None]
        state = jnp.exp(last)[..., None, None] * state + jnp.einsum('bsgn,bsgrp->bgrnp', bi, xs)
        return state, y

    state0 = jnp.zeros((b, g, r, n, p), jnp.float32)
    _, y = lax.scan(step, state0, (_seq_chunks(xdt), _seq_chunks(log_a), _seq_chunks(bm), _seq_chunks(cm)))
    return jnp.moveaxis(y, 0, 1).reshape(b, s, g * r, p)


def _mixer_ab(x, pos, w_in, ret_norm_g, ml_conv_w, ml_conv_b, ml_b_i, ml_b_f, ml_norm_g, w_out):
    f32 = jnp.float32
    proj = x @ w_in
    rq, rk, rv, rg, mq, mk, mv, mo, mi, mf = _split(proj, RET_SPLITS + ML_SPLITS)
    q = _rotary(_heads(rq, RET_H).astype(f32), pos)
    k = _rotary(_heads(rk, RET_H).astype(f32), pos) * RET_DK ** -0.5
    v = _heads(rv, RET_H).astype(f32)
    y_ret = _head_norm(_retention(q, k, v), RET_H, ret_norm_g, True) * jax.nn.silu(rg.astype(f32))
    qk = jax.nn.silu(_causal_conv(jnp.concatenate([mq, mk], axis=-1), ml_conv_w, ml_conv_b))
    cq, ck = jnp.split(qk, 2, axis=-1)
    q = _heads(cq, ML_H).astype(f32)
    k = _heads(ck, ML_H).astype(f32) * ML_DK ** -0.5
    v = _heads(mv, ML_H).astype(f32)
    i_pre = (mi + ml_b_i).astype(f32).transpose(0, 2, 1)
    logf = jax.nn.log_sigmoid((mf + ml_b_f).astype(f32)).transpose(0, 2, 1)
    h = _mlstm(q, k, v, i_pre, logf) * jax.nn.sigmoid(mo.astype(f32))
    y_ml = _head_norm(h, ML_H, ml_norm_g, True)
    y = jnp.concatenate([y_ret, y_ml], axis=-1).astype(x.dtype)
    return y @ w_out


def _mixer_cd(x, w_in, gla_w_alpha, gla_b_alpha, gla_norm_g, ssd_conv_w, ssd_conv_b, ssd_dt_bias, ssd_a_log,
              ssd_d, ssd_norm_g, w_out):
    f32 = jnp.float32
    b, s, _ = x.shape
    proj = x @ w_in
    gq, gk, gv, gr, ga, sz, sxbc, sdt = _split(proj, GLA_SPLITS + SSD_SPLITS)
    log_alpha = jax.nn.log_sigmoid((ga @ gla_w_alpha + gla_b_alpha).astype(f32)) / GLA_TAU
    q = _heads(gq, GLA_H).astype(f32) * GLA_DK ** -0.5
    k = _heads(gk, GLA_H).astype(f32)
    v = _heads(gv, GLA_H).astype(f32)
    y_gla = _head_norm(_gla(q, k, v, _heads(log_alpha, GLA_H)), GLA_H, gla_norm_g, False) * jax.nn.silu(gr.astype(f32))
    xbc = jax.nn.silu(_causal_conv(sxbc, ssd_conv_w, ssd_conv_b)).astype(f32)
    xs, bm, cm = _split(xbc, (SSD_DINNER, SSD_GROUPS * SSD_N, SSD_GROUPS * SSD_N))
    dt = jax.nn.softplus((sdt + ssd_dt_bias).astype(f32))
    a = -jnp.exp(ssd_a_log.astype(f32))
    xs_h = xs.reshape(b, s, SSD_H, SSD_HEADDIM)
    y = _ssd((xs_h * dt[..., None]).reshape(b, s, SSD_GROUPS, SSD_HPG, SSD_HEADDIM),
             (dt * a).reshape(b, s, SSD_GROUPS, SSD_HPG),
             bm.reshape(b, s, SSD_GROUPS, SSD_N), cm.reshape(b, s, SSD_GROUPS, SSD_N))
    y = (y + ssd_d.astype(f32)[:, None] * xs_h).reshape(b, s, SSD_DINNER) * jax.nn.silu(sz.astype(f32))
    y_ssd = _head_norm(y, SSD_GROUPS, ssd_norm_g, False)
    out = jnp.concatenate([y_gla, y_ssd], axis=-1).astype(x.dtype)
    return out @ w_out


def _moe(x, w_router, b_router, w_gate, w_up, w_down):
    b, s, d = x.shape
    xt = x.reshape(b * s, d)
    probs = jax.nn.softmax((xt @ w_router).astype(jnp.float32), axis=-1)
    sel = probs + b_router.astype(jnp.float32)
    grp_score = lax.top_k(sel.reshape(-1, N_EXPERT_GROUPS, EXPERTS_PER_GROUP), GROUP_SCORE_TOPK)[0].sum(-1)
    g_best = jnp.argmax(grp_score, axis=-1)
    in_group = (jnp.arange(N_EXPERTS) // EXPERTS_PER_GROUP)[None, :] == g_best[:, None]
    _, e_idx = lax.top_k(jnp.where(in_group, sel, -jnp.inf), TOP_K)
    gate = jnp.take_along_axis(probs, e_idx, axis=-1)
    gate = gate / gate.sum(-1, keepdims=True)
    combine = jnp.einsum('tk,tke->te', gate, jax.nn.one_hot(e_idx, N_EXPERTS, dtype=jnp.float32))
    y = jnp.zeros((b * s, d), jnp.float32)
    for e in range(N_EXPERTS):
        h = jax.nn.silu(xt @ w_gate[e]) * (xt @ w_up[e])
        y = y + combine[:, e:e + 1] * (h @ w_down[e]).astype(jnp.float32)
    return y.astype(x.dtype).reshape(b, s, d)


def setup_inputs(seed: int = 0) -> dict:
    key = jax.random.key(seed)
    ks = iter(jax.random.split(key, 40))
    f32 = jnp.float32

    def nrm(shape, scale):
        return scale * jax.random.normal(next(ks), shape, f32)

    def gain(shape):
        return 1.0 + nrm(shape, 0.02)

    x = jax.random.normal(next(ks), (BATCH, SEQ, D_MODEL), f32)
    offset = jax.random.randint(next(ks), (BATCH, 1), 0, 4096, dtype=jnp.int32)
    positions = (offset + jnp.arange(SEQ, dtype=jnp.int32)[None, :]).astype(jnp.int32)
    dt0 = jnp.exp(jax.random.uniform(next(ks), (N_ODD, SSD_H), f32, math.log(1e-3), math.log(1e-1)))
    return {
        'x': x,
        'positions': positions,
        'w_in_ab': nrm((N_EVEN, D_MODEL, P_AB), D_MODEL ** -0.5),
        'ret_norm_g': gain((N_EVEN, RET_H * RET_DV)),
        'ml_conv_w': nrm((N_EVEN, CONV_K, 2 * ML_H * ML_DK), CONV_K ** -0.5),
        'ml_conv_b': nrm((N_EVEN, 2 * ML_H * ML_DK), 0.02),
        'ml_b_i': nrm((N_EVEN, ML_H), 0.1),
        'ml_b_f': jnp.linspace(3.0, 6.0, ML_H, dtype=f32)[None, :] + nrm((N_EVEN, ML_H), 0.02),
        'ml_norm_g': gain((N_EVEN, ML_H * ML_DV)),
        'w_out_ab': nrm((N_EVEN, MIX_AB, D_MODEL), DEEPNORM_BETA * MIX_AB ** -0.5),
        'w_in_cd': nrm((N_ODD, D_MODEL, P_CD), D_MODEL ** -0.5),
        'gla_w_alpha': nrm((N_ODD, GLA_RANK, GLA_H * GLA_DK), GLA_RANK ** -0.5),
        'gla_b_alpha': nrm((N_ODD, GLA_H * GLA_DK), 0.02),
        'gla_norm_g': gain((N_ODD, GLA_H * GLA_DV)),
        'ssd_conv_w': nrm((N_ODD, CONV_K, SSD_DINNER + 2 * SSD_GROUPS * SSD_N), CONV_K ** -0.5),
        'ssd_conv_b': nrm((N_ODD, SSD_DINNER + 2 * SSD_GROUPS * SSD_N), 0.02),
        'ssd_dt_bias': dt0 + jnp.log(-jnp.expm1(-dt0)),
        'ssd_a_log': jnp.log(jax.random.uniform(next(ks), (N_ODD, SSD_H), f32, 1.0, 16.0)),
        'ssd_d': gain((N_ODD, SSD_H)),
        'ssd_norm_g': gain((N_ODD, SSD_DINNER)),
        'w_out_cd': nrm((N_ODD, MIX_CD, D_MODEL), DEEPNORM_BETA * MIX_CD ** -0.5),
        'w_router': nrm((D_MODEL, N_EXPERTS), D_MODEL ** -0.5),
        'b_router': nrm((N_EXPERTS,), 0.01),
        'moe_w_gate': nrm((DEPTH, N_EXPERTS, D_MODEL, D_FF), D_MODEL ** -0.5),
        'moe_w_up': nrm((DEPTH, N_EXPERTS, D_MODEL, D_FF), D_MODEL ** -0.5),
        'moe_w_down': nrm((DEPTH, N_EXPERTS, D_FF, D_MODEL), DEEPNORM_BETA * D_FF ** -0.5),
        'ln_mix_g': gain((DEPTH, D_MODEL)),
        'ln_mix_b': nrm((DEPTH, D_MODEL), 0.02),
        'ln_ffn_g': gain((DEPTH, D_MODEL)),
        'ln_ffn_b': nrm((DEPTH, D_MODEL), 0.02),
    }


def reference(x, positions, w_in_ab, ret_norm_g, ml_conv_w, ml_conv_b, ml_b_i, ml_b_f, ml_norm_g, w_out_ab,
              w_in_cd, gla_w_alpha, gla_b_alpha, gla_norm_g, ssd_conv_w, ssd_conv_b, ssd_dt_bias, ssd_a_log, ssd_d,
              ssd_norm_g, w_out_cd, w_router, b_router, moe_w_gate, moe_w_up, moe_w_down,
              ln_mix_g, ln_mix_b, ln_ffn_g, ln_ffn_b):
    h = x
    for layer in range(DEPTH):
        j = layer // 2
        if layer % 2 == 0:
            mix = _mixer_ab(h, positions, w_in_ab[j], ret_norm_g[j], ml_conv_w[j], ml_conv_b[j], ml_b_i[j],
                            ml_b_f[j], ml_norm_g[j], w_out_ab[j])
        else:
            mix = _mixer_cd(h, w_in_cd[j], gla_w_alpha[j], gla_b_alpha[j], gla_norm_g[j], ssd_conv_w[j],
                            ssd_conv_b[j], ssd_dt_bias[j], ssd_a_log[j], ssd_d[j], ssd_norm_g[j], w_out_cd[j])
        h = _layernorm(DEEPNORM_ALPHA * h + mix, ln_mix_g[layer], ln_mix_b[layer])
        ffn = _moe(h, w_router, b_router, moe_w_gate[layer], moe_w_up[layer], moe_w_down[layer])
        h = _layernorm(DEEPNORM_ALPHA * h + ffn, ln_ffn_g[layer], ln_ffn_b[layer])
    return h
```

```python
import functools
import math

import jax
import jax.numpy as jnp
import numpy as np
from jax import lax
from jax.experimental import pallas as pl
from jax.experimental.pallas import tpu as pltpu

F32 = jnp.float32
BF16 = jnp.bfloat16

D_MODEL = 1024
DEPTH = 2
LN_EPS = 1e-5
NORM_EPS = 1e-6
N_HEADS = 4
DK = D_MODEL // 8
DV = D_MODEL // 4
ROPE_BASE = 10000.0
CONV_K = 4
GLA_RANK = 16
GLA_TAU = 16.0
SSD_P = 64
SSD_H = D_MODEL // SSD_P
SSD_G = 4
SSD_HPG = SSD_H // SSD_G
SSD_N = 128
N_EXPERTS = 16
N_GROUPS = 4
EPG = N_EXPERTS // N_GROUPS
D_FF = D_MODEL // 2
ALPHA = (2.0 * DEPTH) ** 0.25

LANES = 128
SUBLANES = 8
VMEM_LIMIT = 56 * 1024 * 1024

P_MAIN = 6 * D_MODEL
NEG_BIG = -1e30


def _cparams(sem):
    return pltpu.CompilerParams(dimension_semantics=sem, vmem_limit_bytes=VMEM_LIMIT)


def _dot(a, b):
    return jnp.dot(a, b, preferred_element_type=F32)


def _dot_nt(a, b):
    return lax.dot_general(a, b, (((1,), (1,)), ((), ())), preferred_element_type=F32)


def _dot_tn(a, b):
    return lax.dot_general(a, b, (((0,), (0,)), ((), ())), preferred_element_type=F32)


def _sigmoid(x):
    return 1.0 / (1.0 + jnp.exp(-x))


def _silu(x):
    return x * _sigmoid(x)


def _log_sigmoid(x):
    return jnp.minimum(x, 0.0) - jnp.log(1.0 + jnp.exp(-jnp.abs(x)))


def _softplus(x):
    return jnp.maximum(x, 0.0) + jnp.log(1.0 + jnp.exp(-jnp.abs(x)))


def _tri_cumsum(tri, x):
    hi = x.astype(BF16)
    r1 = x - hi.astype(F32)
    mid = r1.astype(BF16)
    lo = (r1 - mid.astype(F32)).astype(BF16)
    return _dot(tri, hi) + _dot(tri, mid) + _dot(tri, lo)


def _head_norm(y, g, center):
    if center:
        y = y - jnp.mean(y, axis=-1, keepdims=True)
    return y * lax.rsqrt(jnp.mean(y * y, axis=-1, keepdims=True) + NORM_EPS) * g


def _inproj_kernel(x_ref, wm_ref, ws_ref, om_ref, os_ref, xb_ref):
    @pl.when(pl.program_id(1) == 0)
    def _():
        xb = x_ref[...].astype(BF16)
        xb_ref[...] = xb
        os_ref[...] = _dot(xb, ws_ref[...])

    om_ref[...] = _dot(xb_ref[...], wm_ref[...]).astype(om_ref.dtype)


def _inproj(x, w_main, w_small, tm=1024, tn=1024):
    t, d = x.shape
    tm = min(tm, t)
    n = w_main.shape[1]
    return pl.pallas_call(
        _inproj_kernel,
        grid=(t // tm, n // tn),
        in_specs=[
            pl.BlockSpec((tm, d), lambda i, j: (i, 0)),
            pl.BlockSpec((d, tn), lambda i, j: (0, j)),
            pl.BlockSpec((d, LANES), lambda i, j: (0, 0)),
        ],
        out_specs=[
            pl.BlockSpec((tm, tn), lambda i, j: (i, j)),
            pl.BlockSpec((tm, LANES), lambda i, j: (i, 0)),
        ],
        out_shape=[jax.ShapeDtypeStruct((t, n), BF16), jax.ShapeDtypeStruct((t, LANES), F32)],
        scratch_shapes=[pltpu.VMEM((tm, d), BF16)],
        compiler_params=_cparams(("parallel", "arbitrary")),
        name="inproj",
    )(x, w_main, w_small)


def _rope_kernel(pos_ref, inv_ref, cc_ref, ss_ref):
    ang = pos_ref[...].astype(F32) * inv_ref[...]
    lane = lax.broadcasted_iota(jnp.int32, ang.shape, 1)
    s = jnp.sin(ang)
    cc_ref[...] = jnp.cos(ang)
    ss_ref[...] = jnp.where(lane < DK // 2, -s, s)


def _rope_tables(pos_col, inv2, tm=2048):
    t = pos_col.shape[0]
    tm = min(tm, t)
    return pl.pallas_call(
        _rope_kernel,
        grid=(t // tm,),
        in_specs=[pl.BlockSpec((tm, 1), lambda i: (i, 0)), pl.BlockSpec((1, DK), lambda i: (0, 0))],
        out_specs=[pl.BlockSpec((tm, DK), lambda i: (i, 0))] * 2,
        out_shape=[jax.ShapeDtypeStruct((t, DK), F32)] * 2,
        compiler_params=_cparams(("parallel",)),
        name="rope_tables",
    )(pos_col, inv2)


def _causal_conv_tile(hist_ref, u, w_ref, b_ref, c):
    hist_ref[pl.ds(SUBLANES, c), :] = u
    acc = b_ref[...] + w_ref[CONV_K - 1:CONV_K, :] * u
    for j in range(CONV_K - 1):
        acc = acc + w_ref[j:j + 1, :] * hist_ref[pl.ds(SUBLANES - (CONV_K - 1) + j, c), :]
    hist_ref[pl.ds(0, SUBLANES), :] = hist_ref[pl.ds(c, SUBLANES), :]
    return acc


def _mixer_ab_kernel(p_ref, g_ref, cc_ref, ss_ref, dmask_ref, qdec_ref, kdec_ref, cdec_ref, tri_ref,
                     rng_ref, mng_ref, cw_ref, cb_ref, gb_ref, o_ref,
                     rs_ref, mc_ref, mn_ref, mm_ref, hist_ref, *, c):
    @pl.when(pl.program_id(1) == 0)
    def _():
        rs_ref[...] = jnp.zeros_like(rs_ref)
        mc_ref[...] = jnp.zeros_like(mc_ref)
        mn_ref[...] = jnp.zeros_like(mn_ref)
        mm_ref[...] = jnp.zeros_like(mm_ref)
        hist_ref[pl.ds(0, SUBLANES), :] = jnp.zeros((SUBLANES, hist_ref.shape[1]), F32)

    cc = cc_ref[...]
    ss = ss_ref[...]
    kscale = DK ** -0.5

    def rot(t):
        return t * cc + pltpu.roll(t, DK // 2, 1) * ss

    for h in range(N_HEADS):
        q = rot(p_ref[:, h * DK:(h + 1) * DK].astype(F32))
        k = rot(p_ref[:, 512 + h * DK:512 + (h + 1) * DK].astype(F32)) * kscale
        v = p_ref[:, 1024 + h * DV:1024 + (h + 1) * DV]
        scores = _dot_nt(q.astype(BF16), k.astype(BF16)) * dmask_ref[h]
        st = rs_ref[h]
        y = _dot(scores.astype(BF16), v) + _dot((q * qdec_ref[h]).astype(BF16), st.astype(BF16))
        rs_ref[h] = cdec_ref[h] * st + _dot_tn((k * kdec_ref[h]).astype(BF16), v)
        yn = _head_norm(y, rng_ref[:, h * DV:(h + 1) * DV], True)
        gate = p_ref[:, 2048 + h * DV:2048 + (h + 1) * DV].astype(F32)
        o_ref[:, h * DV:(h + 1) * DV] = (yn * _silu(gate)).astype(o_ref.dtype)

    u = p_ref[:, 3072:4096].astype(F32)
    qk = _silu(_causal_conv_tile(hist_ref, u, cw_ref, cb_ref, c))
    gates = g_ref[...] + gb_ref[...]
    lane = lax.broadcasted_iota(jnp.int32, gates.shape, 1)
    is_f = (lane // N_HEADS) == 1
    logf = jnp.where(is_f, _log_sigmoid(gates), 0.0)
    bcum = _tri_cumsum(tri_ref[...], logf)
    pm = jnp.where(lane < N_HEADS, gates, bcum)
    pmt = pm.T
    row = lax.broadcasted_iota(jnp.int32, (c, c), 0)
    col = lax.broadcasted_iota(jnp.int32, (c, c), 1)
    causal = row >= col
    for h in range(N_HEADS):
        q = qk[:, h * DK:(h + 1) * DK]
        k = qk[:, 512 + h * DK:512 + (h + 1) * DK] * kscale
        v = p_ref[:, 4096 + h * DV:4096 + (h + 1) * DV]
        i_col = pm[:, h:h + 1]
        b_col = pm[:, N_HEADS + h:N_HEADS + h + 1]
        i_row = pmt[h:h + 1, :]
        b_row = pmt[N_HEADS + h:N_HEADS + h + 1, :]
        m_prev = mm_ref[h][:, 0:1]
        a = b_col + m_prev
        d = jnp.where(causal, b_col - b_row + i_row, NEG_BIG)
        m_t = jnp.maximum(a, jnp.max(d, axis=1, keepdims=True))
        w_inter = jnp.exp(a - m_t)
        qb = q.astype(BF16)
        s_qk = _dot_nt(qb, k.astype(BF16)) * jnp.exp(d - m_t)
        cst = mc_ref[h]
        nst = mn_ref[h]
        num = _dot(s_qk.astype(BF16), v) + w_inter * _dot(qb, cst.astype(BF16))
        den = jnp.sum(s_qk, axis=1, keepdims=True) + w_inter * jnp.sum(q * nst, axis=1, keepdims=True)
        hh = num / jnp.maximum(jnp.abs(den), jnp.exp(-m_t))
        b_tot = b_col[c - 1:c, :]
        lw = b_tot - b_col + i_col
        m_new = jnp.maximum(b_tot + m_prev, jnp.max(lw, axis=0, keepdims=True))
        ks = k * jnp.exp(lw - m_new)
        decay = jnp.exp(b_tot + m_prev - m_new)
        mc_ref[h] = decay * cst + _dot_tn(ks.astype(BF16), v)
        mn_ref[h] = decay * nst + jnp.sum(ks, axis=0, keepdims=True)
        mm_ref[h] = jnp.broadcast_to(m_new, (1, LANES))
        og = p_ref[:, 5120 + h * DV:5120 + (h + 1) * DV].astype(F32)
        yn = _head_norm(hh * _sigmoid(og), mng_ref[:, h * DV:(h + 1) * DV], True)
        o_ref[:, 1024 + h * DV:1024 + (h + 1) * DV] = yn.astype(o_ref.dtype)


def _mixer_ab(proj, gates, cc, ss, tabs, ret_g, ml_g, conv_w, conv_b, gate_b, b, s, c):
    nt = s // c
    t = b * s
    tok = lambda bi, i: (bi * nt + i, 0)
    full2 = lambda bi, i: (0, 0)
    full3 = lambda bi, i: (0, 0, 0)
    dmask, qdec, kdec, cdec, tri = tabs
    return pl.pallas_call(
        functools.partial(_mixer_ab_kernel, c=c),
        grid=(b, nt),
        in_specs=[
            pl.BlockSpec((c, P_MAIN), tok),
            pl.BlockSpec((c, LANES), tok),
            pl.BlockSpec((c, DK), tok),
            pl.BlockSpec((c, DK), tok),
            pl.BlockSpec((N_HEADS, c, c), full3),
            pl.BlockSpec((N_HEADS, c, DK), full3),
            pl.BlockSpec((N_HEADS, c, DK), full3),
            pl.BlockSpec((N_HEADS, 1, DV), full3),
            pl.BlockSpec((c, c), full2),
            pl.BlockSpec((1, D_MODEL), full2),
            pl.BlockSpec((1, D_MODEL), full2),
            pl.BlockSpec((CONV_K, D_MODEL), full2),
            pl.BlockSpec((1, D_MODEL), full2),
            pl.BlockSpec((1, LANES), full2),
        ],
        out_specs=pl.BlockSpec((c, 2 * D_MODEL), tok),
        out_shape=jax.ShapeDtypeStruct((t, 2 * D_MODEL), BF16),
        scratch_shapes=[
            pltpu.VMEM((N_HEADS, DK, DV), F32),
            pltpu.VMEM((N_HEADS, DK, DV), F32),
            pltpu.VMEM((N_HEADS, 1, DK), F32),
            pltpu.VMEM((N_HEADS, 1, LANES), F32),
            pltpu.VMEM((SUBLANES + c, D_MODEL), F32),
        ],
        compiler_params=_cparams(("parallel", "arbitrary")),
        name="mixer_ab",
    )(proj, gates, cc, ss, dmask, qdec, kdec, cdec, tri, ret_g, ml_g, conv_w, conv_b, gate_b)


def _mixer_cd_kernel(p_ref, g_ref, tri_ref, trig_ref, wal_ref, bal_ref, gng_ref, cw_ref, cb_ref,
                     dtb_ref, alog_ref, dskip_ref, sng_ref, o_ref,
                     gs_ref, ss_ref, hist_ref, *, c, cg):
    @pl.when(pl.program_id(1) == 0)
    def _():
        gs_ref[...] = jnp.zeros_like(gs_ref)
        ss_ref[...] = jnp.zeros_like(ss_ref)
        hist_ref[pl.ds(0, SUBLANES), :] = jnp.zeros((SUBLANES, hist_ref.shape[1]), F32)

    small = g_ref[...]

    log_alpha = _log_sigmoid(_dot(small.astype(BF16), wal_ref[...]) + bal_ref[...]) * (1.0 / GLA_TAU)
    qscale = DK ** -0.5
    trig = trig_ref[...]
    rowg = lax.broadcasted_iota(jnp.int32, (cg, cg), 0)
    colg = lax.broadcasted_iota(jnp.int32, (cg, cg), 1)
    causal_g = rowg >= colg
    for h in range(N_HEADS):
        st = gs_ref[h]
        ys = []
        for j in range(c // cg):
            sl = slice(j * cg, (j + 1) * cg)
            q = p_ref[sl, h * DK:(h + 1) * DK].astype(F32) * qscale
            k = p_ref[sl, 512 + h * DK:512 + (h + 1) * DK].astype(F32)
            v = p_ref[sl, 1024 + h * DV:1024 + (h + 1) * DV]
            bcum = _tri_cumsum(trig, log_alpha[sl, h * DK:(h + 1) * DK])
            qe = (q * jnp.exp(bcum)).astype(BF16)
            scores = jnp.where(causal_g, _dot_nt(qe, (k * jnp.exp(-bcum)).astype(BF16)), 0.0)
            ys.append(_dot(scores.astype(BF16), v) + _dot_nt(qe, st.astype(BF16)))
            b_tot = bcum[cg - 1:cg, :]
            kd = (k * jnp.exp(b_tot - bcum)).astype(BF16)
            st = jnp.exp(b_tot) * st + _dot_tn(v, kd)
        gs_ref[h] = st
        y = jnp.concatenate(ys, axis=0)
        yn = _head_norm(y, gng_ref[:, h * DV:(h + 1) * DV], False)
        gate = p_ref[:, 2048 + h * DV:2048 + (h + 1) * DV].astype(F32)
        o_ref[:, h * DV:(h + 1) * DV] = (yn * _silu(gate)).astype(o_ref.dtype)

    u = p_ref[:, 4096:6144].astype(F32)
    xbc = _silu(_causal_conv_tile(hist_ref, u, cw_ref, cb_ref, c))
    lane = lax.broadcasted_iota(jnp.int32, small.shape, 1)
    is_dt = (lane // GLA_RANK) == 1
    dt = jnp.where(is_dt, _softplus(small + dtb_ref[...]), 0.0)
    la = dt * -jnp.exp(alog_ref[...])
    cum = _tri_cumsum(tri_ref[...], la)
    pm = jnp.where(lane < GLA_RANK, _dt_shift(dt), cum)
    pmt = pm.T
    row = lax.broadcasted_iota(jnp.int32, (c, c), 0)
    col = lax.broadcasted_iota(jnp.int32, (c, c), 1)
    causal = row >= col
    for g in range(SSD_G):
        bm = xbc[:, 1024 + g * SSD_N:1024 + (g + 1) * SSD_N].astype(BF16)
        cm = xbc[:, 1536 + g * SSD_N:1536 + (g + 1) * SSD_N].astype(BF16)
        cb = _dot_nt(cm, bm)
        stg = ss_ref[g]
        y_inter = _dot(cm, stg.astype(BF16))
        y_parts, xs_parts, dec_parts = [], [], []
        for r in range(SSD_HPG):
            hd = g * SSD_HPG + r
            x_h = xbc[:, hd * SSD_P:(hd + 1) * SSD_P]
            dt_col = pm[:, hd:hd + 1]
            cum_col = pm[:, GLA_RANK + hd:GLA_RANK + hd + 1]
            cum_row = pmt[GLA_RANK + hd:GLA_RANK + hd + 1, :]
            seg = jnp.where(causal, cum_col - cum_row, NEG_BIG)
            w = cb * jnp.exp(seg)
            y_h = _dot(w.astype(BF16), (x_h * dt_col).astype(BF16))
            y_h = y_h + jnp.exp(cum_col) * y_inter[:, r * SSD_P:(r + 1) * SSD_P]
            y_parts.append(y_h + dskip_ref[:, hd * SSD_P:(hd + 1) * SSD_P] * x_h)
            last = cum_col[c - 1:c, :]
            xs_parts.append(x_h * (dt_col * jnp.exp(last - cum_col)))
            dec_parts.append(jnp.broadcast_to(jnp.exp(last), (1, SSD_P)))
        xs = jnp.concatenate(xs_parts, axis=1).astype(BF16)
        dec = jnp.concatenate(dec_parts, axis=1)
        ss_ref[g] = dec * stg + _dot_tn(bm, xs)
        yg = jnp.concatenate(y_parts, axis=1) * _silu(p_ref[:, 3072 + g * DV:3072 + (g + 1) * DV].astype(F32))
        o_ref[:, 1024 + g * DV:1024 + (g + 1) * DV] = _head_norm(
            yg, sng_ref[:, g * DV:(g + 1) * DV], False).astype(o_ref.dtype)


def _dt_shift(dt):
    return pltpu.roll(dt, LANES - GLA_RANK, 1)


def _mixer_cd(proj, small, tri, trig, wal, bal, gla_g, conv_w, conv_b, dtb, alog, dskip, ssd_g, b, s, c, cg):
    nt = s // c
    t = b * s
    tok = lambda bi, i: (bi * nt + i, 0)
    full2 = lambda bi, i: (0, 0)
    ch = D_MODEL + 2 * SSD_G * SSD_N
    return pl.pallas_call(
        functools.partial(_mixer_cd_kernel, c=c, cg=cg),
        grid=(b, nt),
        in_specs=[
            pl.BlockSpec((c, P_MAIN), tok),
            pl.BlockSpec((c, LANES), tok),
            pl.BlockSpec((c, c), full2),
            pl.BlockSpec((cg, cg), full2),
            pl.BlockSpec((LANES, N_HEADS * DK), full2),
            pl.BlockSpec((1, N_HEADS * DK), full2),
            pl.BlockSpec((1, D_MODEL), full2),
            pl.BlockSpec((CONV_K, ch), full2),
            pl.BlockSpec((1, ch), full2),
            pl.BlockSpec((1, LANES), full2),
            pl.BlockSpec((1, LANES), full2),
            pl.BlockSpec((1, D_MODEL), full2),
            pl.BlockSpec((1, D_MODEL), full2),
        ],
        out_specs=pl.BlockSpec((c, 2 * D_MODEL), tok),
        out_shape=jax.ShapeDtypeStruct((t, 2 * D_MODEL), BF16),
        scratch_shapes=[
            pltpu.VMEM((N_HEADS, DV, DK), F32),
            pltpu.VMEM((SSD_G, SSD_N, SSD_HPG * SSD_P), F32),
            pltpu.VMEM((SUBLANES + c, ch), F32),
        ],
        compiler_params=_cparams(("parallel", "arbitrary")),
        name="mixer_cd",
    )(proj, small, tri, trig, wal, bal, gla_g, conv_w, conv_b, dtb, alog, dskip, ssd_g)


def _layernorm(z, g, b):
    zc = z - jnp.mean(z, axis=-1, keepdims=True)
    return zc * lax.rsqrt(jnp.mean(zc * zc, axis=-1, keepdims=True) + LN_EPS) * g + b


def _outproj_ln_kernel(y_ref, w_ref, h_ref, g_ref, b_ref, o_ref):
    z = ALPHA * h_ref[...] + _dot(y_ref[...], w_ref[...])
    o_ref[...] = _layernorm(z, g_ref[...], b_ref[...])


def _outproj_ln(y, w, h, g, b, tm=512):
    t, kdim = y.shape
    tm = min(tm, t)
    d = h.shape[1]
    return pl.pallas_call(
        _outproj_ln_kernel,
        grid=(t // tm,),
        in_specs=[
            pl.BlockSpec((tm, kdim), lambda i: (i, 0)),
            pl.BlockSpec((kdim, d), lambda i: (0, 0)),
            pl.BlockSpec((tm, d), lambda i: (i, 0)),
            pl.BlockSpec((1, d), lambda i: (0, 0)),
            pl.BlockSpec((1, d), lambda i: (0, 0)),
        ],
        out_specs=pl.BlockSpec((tm, d), lambda i: (i, 0)),
        out_shape=jax.ShapeDtypeStruct((t, d), F32),
        compiler_params=_cparams(("parallel",)),
        name="outproj_ln",
    )(y, w, h, g, b)


def _router_kernel(h_ref, w_ref, b_ref, o_ref):
    logits = jnp.dot(h_ref[...], w_ref[...], preferred_element_type=F32, precision=lax.Precision.HIGHEST)
    rep = LANES // N_EXPERTS
    e = jnp.exp(logits - jnp.max(logits, axis=1, keepdims=True))
    probs = e / (jnp.sum(e, axis=1, keepdims=True) * (1.0 / rep))
    sel = probs + b_ref[...]
    lane = lax.broadcasted_iota(jnp.int32, sel.shape, 1)
    pos = lane & (EPG - 1)
    gidx = (lane // EPG) & (N_GROUPS - 1)

    def member(a, k):
        fwd = pltpu.roll(a, LANES - k, 1)
        back = pltpu.roll(a, EPG - k, 1)
        return jnp.where(pos + k < EPG, fwd, back)

    others = [member(sel, k) for k in range(1, EPG)]
    vals = [sel] + others
    top2 = None
    for i in range(EPG):
        for j in range(i + 1, EPG):
            pair = vals[i] + vals[j]
            top2 = pair if top2 is None else jnp.maximum(top2, pair)
    best = jnp.ones(sel.shape, jnp.int32)
    for k in range(1, N_GROUPS):
        other = pltpu.roll(top2, EPG * k, 1)
        wins = jnp.where(gidx >= k, jnp.where(top2 > other, 1, 0), jnp.where(top2 >= other, 1, 0))
        best = best * wins
    rank = jnp.zeros(sel.shape, jnp.int32)
    for k in range(1, EPG):
        v = others[k - 1]
        tie = jnp.where(pos + k >= EPG, 1, 0)
        rank = rank + jnp.where(v > sel, 1, jnp.where(v == sel, tie, 0))
    gsel = jnp.where(best * jnp.where(rank < 2, 1, 0) > 0, probs, 0.0)
    comb = gsel / (jnp.sum(gsel, axis=1, keepdims=True) * (1.0 / rep))
    o_ref[...] = comb[:, :N_EXPERTS]


def _router(h, w_rep, b_rep, tm=1024):
    t, d = h.shape
    tm = min(tm, t)
    return pl.pallas_call(
        _router_kernel,
        grid=(t // tm,),
        in_specs=[
            pl.BlockSpec((tm, d), lambda i: (i, 0)),
            pl.BlockSpec((d, LANES), lambda i: (0, 0)),
            pl.BlockSpec((1, LANES), lambda i: (0, 0)),
        ],
        out_specs=pl.BlockSpec((tm, N_EXPERTS), lambda i: (i, 0)),
        out_shape=jax.ShapeDtypeStruct((t, N_EXPERTS), F32),
        compiler_params=_cparams(("parallel",)),
        name="router",
    )(h, w_rep, b_rep)


def _moe_kernel(h_ref, c_ref, wg_ref, wu_ref, wd_ref, g_ref, b_ref, o_ref, hb_ref, acc_ref):
    e = pl.program_id(1)

    @pl.when(e == 0)
    def _():
        hb_ref[...] = h_ref[...].astype(BF16)
        acc_ref[...] = jnp.zeros_like(acc_ref)

    hb = hb_ref[...]
    act = _silu(_dot(hb, wg_ref[...])) * _dot(hb, wu_ref[...])
    cmb = c_ref[...]
    lane = lax.broadcasted_iota(jnp.int32, cmb.shape, 1)
    ce = jnp.sum(jnp.where(lane == e, cmb, 0.0), axis=1, keepdims=True)
    acc_ref[...] += ce * _dot(act.astype(BF16), wd_ref[...])

    @pl.when(e == N_EXPERTS - 1)
    def _():
        o_ref[...] = _layernorm(ALPHA * h_ref[...] + acc_ref[...], g_ref[...], b_ref[...])


def _moe_ln(h, comb, wg, wu, wd, g, b, tm=1024):
    t, d = h.shape
    tm = min(tm, t)
    return pl.pallas_call(
        _moe_kernel,
        grid=(t // tm, N_EXPERTS),
        in_specs=[
            pl.BlockSpec((tm, d), lambda i, e: (i, 0)),
            pl.BlockSpec((tm, N_EXPERTS), lambda i, e: (i, 0)),
            pl.BlockSpec((None, d, D_FF), lambda i, e: (e, 0, 0)),
            pl.BlockSpec((None, d, D_FF), lambda i, e: (e, 0, 0)),
            pl.BlockSpec((None, D_FF, d), lambda i, e: (e, 0, 0)),
            pl.BlockSpec((1, d), lambda i, e: (0, 0)),
            pl.BlockSpec((1, d), lambda i, e: (0, 0)),
        ],
        out_specs=pl.BlockSpec((tm, d), lambda i, e: (i, 0)),
        out_shape=jax.ShapeDtypeStruct((t, d), F32),
        scratch_shapes=[pltpu.VMEM((tm, d), BF16), pltpu.VMEM((tm, d), F32)],
        compiler_params=_cparams(("parallel", "arbitrary")),
        name="moe_ln",
    )(h, comb, wg, wu, wd, g, b)


def _retention_tables(c):
    log_g = np.log(1.0 - 2.0 ** (-5.0 - np.arange(N_HEADS, dtype=np.float32))).astype(np.float32)
    idx = np.arange(c, dtype=np.float32)
    diff = idx[:, None] - idx[None, :]
    dmask = np.where(diff >= 0, np.exp(log_g[:, None, None] * np.maximum(diff, 0.0)), 0.0)
    qdec = np.exp(log_g[:, None] * (idx + 1.0))[..., None] * np.ones((1, 1, DK), np.float32)
    kdec = np.exp(log_g[:, None] * (c - 1.0 - idx))[..., None] * np.ones((1, 1, DK), np.float32)
    cdec = np.exp(log_g * c)[:, None, None] * np.ones((1, 1, DV), np.float32)
    return tuple(jnp.asarray(a, F32) for a in (dmask, qdec, kdec, cdec))


def _tri(c):
    return jnp.asarray(np.tril(np.ones((c, c), np.float32)), BF16)


def _pad_cols(w, n):
    return jnp.pad(w, ((0, 0), (0, n - w.shape[1])))


def kernel(x, positions, w_in_ab, ret_norm_g, ml_conv_w, ml_conv_b, ml_b_i, ml_b_f, ml_norm_g, w_out_ab,
           w_in_cd, gla_w_alpha, gla_b_alpha, gla_norm_g, ssd_conv_w, ssd_conv_b, ssd_dt_bias, ssd_a_log, ssd_d,
           ssd_norm_g, w_out_cd, w_router, b_router, moe_w_gate, moe_w_up, moe_w_down,
           ln_mix_g, ln_mix_b, ln_ffn_g, ln_ffn_b):
    b, s, d = x.shape
    t = b * s
    c = min(256, s)
    cg = min(64, c)
    h = x.reshape(t, d)
    row = lambda v: v.reshape(1, -1).astype(F32)

    w_rep = jnp.tile(w_router.astype(F32), (1, LANES // N_EXPERTS))
    b_rep = jnp.tile(b_router.astype(F32).reshape(1, -1), (1, LANES // N_EXPERTS))
    wg = moe_w_gate.astype(BF16)
    wu = moe_w_up.astype(BF16)
    wd = moe_w_down.astype(BF16)

    def ffn(hh, layer):
        comb = _router(hh, w_rep, b_rep)
        return _moe_ln(hh, comb, wg[layer], wu[layer], wd[layer], row(ln_ffn_g[layer]), row(ln_ffn_b[layer]))

    half = np.concatenate([np.arange(0, DK, 2), np.arange(1, DK, 2)])
    perm = np.concatenate([hd * DK + half for hd in range(N_HEADS)])
    w = w_in_ab[0]
    nqk = N_HEADS * DK
    w_main = jnp.concatenate([w[:, :nqk][:, perm], w[:, nqk:2 * nqk][:, perm], w[:, 2 * nqk:P_MAIN]], axis=1)
    proj, gates = _inproj(h, w_main.astype(BF16), _pad_cols(w[:, P_MAIN:], LANES).astype(BF16))
    inv = ROPE_BASE ** (-jnp.arange(0, DK, 2, dtype=F32) / DK)
    cc, ss = _rope_tables(positions.reshape(t, 1), jnp.concatenate([inv, inv]).reshape(1, DK))
    gate_b = _pad_cols(jnp.concatenate([ml_b_i[0], ml_b_f[0]]).reshape(1, -1).astype(F32), LANES)
    y = _mixer_ab(proj, gates, cc, ss, _retention_tables(c) + (_tri(c),), row(ret_norm_g[0]), row(ml_norm_g[0]),
                  ml_conv_w[0].astype(F32), row(ml_conv_b[0]), gate_b, b, s, c)
    h = _outproj_ln(y, w_out_ab[0].astype(BF16), h, row(ln_mix_g[0]), row(ln_mix_b[0]))
    h = ffn(h, 0)

    w = w_in_cd[0]
    g0 = 2 * nqk + 2 * N_HEADS * DV
    s0 = g0 + GLA_RANK
    s1 = s0 + D_MODEL + D_MODEL + 2 * SSD_G * SSD_N
    w_main = jnp.concatenate([w[:, :g0], w[:, s0:s1]], axis=1)
    w_small = _pad_cols(jnp.concatenate([w[:, g0:s0], w[:, s1:]], axis=1), LANES)
    proj, small = _inproj(h, w_main.astype(BF16), w_small.astype(BF16))
    wal = jnp.pad(gla_w_alpha[0], ((0, LANES - GLA_RANK), (0, 0))).astype(BF16)
    lane_pad = lambda v: jnp.pad(v.reshape(1, -1).astype(F32), ((0, 0), (GLA_RANK, LANES - GLA_RANK - SSD_H)))
    dskip = jnp.repeat(ssd_d[0].astype(F32), SSD_P).reshape(1, -1)
    y = _mixer_cd(proj, small, _tri(c), _tri(cg), wal, row(gla_b_alpha[0]), row(gla_norm_g[0]),
                  ssd_conv_w[0].astype(F32), row(ssd_conv_b[0]), lane_pad(ssd_dt_bias[0]), lane_pad(ssd_a_log[0]),
                  dskip, row(ssd_norm_g[0]), b, s, c, cg)
    h = _outproj_ln(y, w_out_cd[0].astype(BF16), h, row(ln_mix_g[1]), row(ln_mix_b[1]))
    h = ffn(h, 1)
    return h.reshape(b, s, d)
```

```python
import functools
import math

import jax
import jax.numpy as jnp
import numpy as np
from jax import lax
from jax.experimental import pallas as pl
from jax.experimental.pallas import tpu as pltpu

F32 = jnp.float32
BF16 = jnp.bfloat16

D_MODEL = 1024
DEPTH = 2
LN_EPS = 1e-5
NORM_EPS = 1e-6
N_HEADS = 4
DK = D_MODEL // 8
DV = D_MODEL // 4
ROPE_BASE = 10000.0
CONV_K = 4
GLA_RANK = 16
GLA_TAU = 16.0
SSD_P = 64
SSD_H = D_MODEL // SSD_P
SSD_G = 4
SSD_HPG = SSD_H // SSD_G
SSD_N = 128
N_EXPERTS = 16
N_GROUPS = 4
EPG = N_EXPERTS // N_GROUPS
D_FF = D_MODEL // 2
ALPHA = (2.0 * DEPTH) ** 0.25

LANES = 128
SUBLANES = 8
VMEM_LIMIT = 56 * 1024 * 1024

P_MAIN = 6 * D_MODEL
TM_F = 512
NEG_BIG = -1e30


def _cparams(sem):
    return pltpu.CompilerParams(dimension_semantics=sem, vmem_limit_bytes=VMEM_LIMIT)


def _dot(a, b):
    return jnp.dot(a, b, preferred_element_type=F32)


def _dot_nt(a, b):
    return lax.dot_general(a, b, (((1,), (1,)), ((), ())), preferred_element_type=F32)


def _dot_tn(a, b):
    return lax.dot_general(a, b, (((0,), (0,)), ((), ())), preferred_element_type=F32)


def _sigmoid(x):
    return 1.0 / (1.0 + jnp.exp(-x))


def _silu(x):
    return x * _sigmoid(x)


def _log_sigmoid(x):
    return jnp.minimum(x, 0.0) - jnp.log(1.0 + jnp.exp(-jnp.abs(x)))


def _softplus(x):
    return jnp.maximum(x, 0.0) + jnp.log(1.0 + jnp.exp(-jnp.abs(x)))


def _tri_cumsum(tri, x):
    hi = x.astype(BF16)
    r1 = x - hi.astype(F32)
    mid = r1.astype(BF16)
    lo = (r1 - mid.astype(F32)).astype(BF16)
    return _dot(tri, hi) + _dot(tri, mid) + _dot(tri, lo)


def _head_norm(y, g, center):
    if center:
        y = y - jnp.mean(y, axis=-1, keepdims=True)
    return y * lax.rsqrt(jnp.mean(y * y, axis=-1, keepdims=True) + NORM_EPS) * g


def _inproj_kernel(x_ref, wm_ref, ws_ref, om_ref, os_ref, xb_ref):
    @pl.when(pl.program_id(1) == 0)
    def _():
        xb = x_ref[...].astype(BF16)
        xb_ref[...] = xb
        os_ref[...] = _dot(xb, ws_ref[...])

    om_ref[...] = _dot(xb_ref[...], wm_ref[...]).astype(om_ref.dtype)


def _inproj(x, w_main, w_small, tm=1024, tn=1024):
    t, d = x.shape
    tm = min(tm, t)
    n = w_main.shape[1]
    return pl.pallas_call(
        _inproj_kernel,
        grid=(t // tm, n // tn),
        in_specs=[
            pl.BlockSpec((tm, d), lambda i, j: (i, 0)),
            pl.BlockSpec((d, tn), lambda i, j: (0, j)),
            pl.BlockSpec((d, LANES), lambda i, j: (0, 0)),
        ],
        out_specs=[
            pl.BlockSpec((tm, tn), lambda i, j: (i, j)),
            pl.BlockSpec((tm, LANES), lambda i, j: (i, 0)),
        ],
        out_shape=[jax.ShapeDtypeStruct((t, n), BF16), jax.ShapeDtypeStruct((t, LANES), F32)],
        scratch_shapes=[pltpu.VMEM((tm, d), BF16)],
        compiler_params=_cparams(("parallel", "arbitrary")),
        name="inproj",
    )(x, w_main, w_small)


def _rope_kernel(pos_ref, inv_ref, cc_ref, ss_ref):
    ang = pos_ref[...].astype(F32) * inv_ref[...]
    lane = lax.broadcasted_iota(jnp.int32, ang.shape, 1)
    s = jnp.sin(ang)
    cc_ref[...] = jnp.cos(ang)
    ss_ref[...] = jnp.where(lane < DK // 2, -s, s)


def _rope_tables(pos_col, inv2, tm=2048):
    t = pos_col.shape[0]
    tm = min(tm, t)
    return pl.pallas_call(
        _rope_kernel,
        grid=(t // tm,),
        in_specs=[pl.BlockSpec((tm, 1), lambda i: (i, 0)), pl.BlockSpec((1, DK), lambda i: (0, 0))],
        out_specs=[pl.BlockSpec((tm, DK), lambda i: (i, 0))] * 2,
        out_shape=[jax.ShapeDtypeStruct((t, DK), F32)] * 2,
        compiler_params=_cparams(("parallel",)),
        name="rope_tables",
    )(pos_col, inv2)


def _causal_conv_tile(hist_ref, u, w_ref, b_ref, c):
    hist_ref[pl.ds(SUBLANES, c), :] = u
    acc = b_ref[...] + w_ref[CONV_K - 1:CONV_K, :] * u
    for j in range(CONV_K - 1):
        acc = acc + w_ref[j:j + 1, :] * hist_ref[pl.ds(SUBLANES - (CONV_K - 1) + j, c), :]
    hist_ref[pl.ds(0, SUBLANES), :] = hist_ref[pl.ds(c, SUBLANES), :]
    return acc


def _mixer_ab_kernel(p_ref, g_ref, cc_ref, ss_ref, dmask_ref, qdec_ref, kdec_ref, cdec_ref, tri_ref,
                     rng_ref, mng_ref, cw_ref, cb_ref, gb_ref, o_ref,
                     rs_ref, mc_ref, mn_ref, mm_ref, hist_ref, *, c):
    @pl.when(pl.program_id(1) == 0)
    def _():
        rs_ref[...] = jnp.zeros_like(rs_ref)
        mc_ref[...] = jnp.zeros_like(mc_ref)
        mn_ref[...] = jnp.zeros_like(mn_ref)
        mm_ref[...] = jnp.zeros_like(mm_ref)
        hist_ref[pl.ds(0, SUBLANES), :] = jnp.zeros((SUBLANES, hist_ref.shape[1]), F32)

    cc = cc_ref[...]
    ss = ss_ref[...]
    kscale = DK ** -0.5

    def rot(t):
        return t * cc + pltpu.roll(t, DK // 2, 1) * ss

    for h in range(N_HEADS):
        q = rot(p_ref[:, h * DK:(h + 1) * DK].astype(F32))
        k = rot(p_ref[:, 512 + h * DK:512 + (h + 1) * DK].astype(F32)) * kscale
        v = p_ref[:, 1024 + h * DV:1024 + (h + 1) * DV]
        scores = _dot_nt(q.astype(BF16), k.astype(BF16)) * dmask_ref[h]
        st = rs_ref[h]
        y = _dot(scores.astype(BF16), v) + _dot((q * qdec_ref[h]).astype(BF16), st.astype(BF16))
        rs_ref[h] = cdec_ref[h] * st + _dot_tn((k * kdec_ref[h]).astype(BF16), v)
        yn = _head_norm(y, rng_ref[:, h * DV:(h + 1) * DV], True)
        gate = p_ref[:, 2048 + h * DV:2048 + (h + 1) * DV].astype(F32)
        o_ref[:, h * DV:(h + 1) * DV] = (yn * _silu(gate)).astype(o_ref.dtype)

    u = p_ref[:, 3072:4096].astype(F32)
    qk = _silu(_causal_conv_tile(hist_ref, u, cw_ref, cb_ref, c))
    gates = g_ref[...] + gb_ref[...]
    lane = lax.broadcasted_iota(jnp.int32, gates.shape, 1)
    is_f = (lane // N_HEADS) == 1
    logf = jnp.where(is_f, _log_sigmoid(gates), 0.0)
    bcum = _tri_cumsum(tri_ref[...], logf)
    pm = jnp.where(lane < N_HEADS, gates, bcum)
    pmt = pm.T
    row = lax.broadcasted_iota(jnp.int32, (c, c), 0)
    col = lax.broadcasted_iota(jnp.int32, (c, c), 1)
    causal = row >= col
    for h in range(N_HEADS):
        q = qk[:, h * DK:(h + 1) * DK]
        k = qk[:, 512 + h * DK:512 + (h + 1) * DK] * kscale
        v = p_ref[:, 4096 + h * DV:4096 + (h + 1) * DV]
        i_col = pm[:, h:h + 1]
        b_col = pm[:, N_HEADS + h:N_HEADS + h + 1]
        i_row = pmt[h:h + 1, :]
        b_row = pmt[N_HEADS + h:N_HEADS + h + 1, :]
        m_prev = mm_ref[h][:, 0:1]
        a = b_col + m_prev
        d = jnp.where(causal, b_col - b_row + i_row, NEG_BIG)
        m_t = jnp.maximum(a, jnp.max(d, axis=1, keepdims=True))
        w_inter = jnp.exp(a - m_t)
        qb = q.astype(BF16)
        s_qk = _dot_nt(qb, k.astype(BF16)) * jnp.exp(d - m_t)
        cst = mc_ref[h]
        nst = mn_ref[h]
        num = _dot(s_qk.astype(BF16), v) + w_inter * _dot(qb, cst.astype(BF16))
        den = jnp.sum(s_qk, axis=1, keepdims=True) + w_inter * jnp.sum(q * nst, axis=1, keepdims=True)
        hh = num / jnp.maximum(jnp.abs(den), jnp.exp(-m_t))
        b_tot = b_col[c - 1:c, :]
        lw = b_tot - b_col + i_col
        m_new = jnp.maximum(b_tot + m_prev, jnp.max(lw, axis=0, keepdims=True))
        ks = k * jnp.exp(lw - m_new)
        decay = jnp.exp(b_tot + m_prev - m_new)
        mc_ref[h] = decay * cst + _dot_tn(ks.astype(BF16), v)
        mn_ref[h] = decay * nst + jnp.sum(ks, axis=0, keepdims=True)
        mm_ref[h] = jnp.broadcast_to(m_new, (1, LANES))
        og = p_ref[:, 5120 + h * DV:5120 + (h + 1) * DV].astype(F32)
        yn = _head_norm(hh * _sigmoid(og), mng_ref[:, h * DV:(h + 1) * DV], True)
        o_ref[:, 1024 + h * DV:1024 + (h + 1) * DV] = yn.astype(o_ref.dtype)


def _mixer_ab(proj, gates, cc, ss, tabs, ret_g, ml_g, conv_w, conv_b, gate_b, b, s, c):
    nt = s // c
    t = b * s
    tok = lambda bi, i: (bi * nt + i, 0)
    full2 = lambda bi, i: (0, 0)
    full3 = lambda bi, i: (0, 0, 0)
    dmask, qdec, kdec, cdec, tri = tabs
    return pl.pallas_call(
        functools.partial(_mixer_ab_kernel, c=c),
        grid=(b, nt),
        in_specs=[
            pl.BlockSpec((c, P_MAIN), tok),
            pl.BlockSpec((c, LANES), tok),
            pl.BlockSpec((c, DK), tok),
            pl.BlockSpec((c, DK), tok),
            pl.BlockSpec((N_HEADS, c, c), full3),
            pl.BlockSpec((N_HEADS, c, DK), full3),
            pl.BlockSpec((N_HEADS, c, DK), full3),
            pl.BlockSpec((N_HEADS, 1, DV), full3),
            pl.BlockSpec((c, c), full2),
            pl.BlockSpec((1, D_MODEL), full2),
            pl.BlockSpec((1, D_MODEL), full2),
            pl.BlockSpec((CONV_K, D_MODEL), full2),
            pl.BlockSpec((1, D_MODEL), full2),
            pl.BlockSpec((1, LANES), full2),
        ],
        out_specs=pl.BlockSpec((c, 2 * D_MODEL), tok),
        out_shape=jax.ShapeDtypeStruct((t, 2 * D_MODEL), BF16),
        scratch_shapes=[
            pltpu.VMEM((N_HEADS, DK, DV), F32),
            pltpu.VMEM((N_HEADS, DK, DV), F32),
            pltpu.VMEM((N_HEADS, 1, DK), F32),
            pltpu.VMEM((N_HEADS, 1, LANES), F32),
            pltpu.VMEM((SUBLANES + c, D_MODEL), F32),
        ],
        compiler_params=_cparams(("parallel", "arbitrary")),
        name="mixer_ab",
    )(proj, gates, cc, ss, dmask, qdec, kdec, cdec, tri, ret_g, ml_g, conv_w, conv_b, gate_b)


def _mixer_cd_kernel(p_ref, g_ref, tri_ref, trig_ref, wal_ref, bal_ref, gng_ref, cw_ref, cb_ref,
                     dtb_ref, alog_ref, dskip_ref, sng_ref, o_ref,
                     gs_ref, ss_ref, hist_ref, *, c, cg):
    @pl.when(pl.program_id(1) == 0)
    def _():
        gs_ref[...] = jnp.zeros_like(gs_ref)
        ss_ref[...] = jnp.zeros_like(ss_ref)
        hist_ref[pl.ds(0, SUBLANES), :] = jnp.zeros((SUBLANES, hist_ref.shape[1]), F32)

    small = g_ref[...]

    log_alpha = _log_sigmoid(_dot(small.astype(BF16), wal_ref[...]) + bal_ref[...]) * (1.0 / GLA_TAU)
    qscale = DK ** -0.5
    trig = trig_ref[...]
    rowg = lax.broadcasted_iota(jnp.int32, (cg, cg), 0)
    colg = lax.broadcasted_iota(jnp.int32, (cg, cg), 1)
    causal_g = rowg >= colg
    for h in range(N_HEADS):
        st = gs_ref[h]
        ys = []
        for j in range(c // cg):
            sl = slice(j * cg, (j + 1) * cg)
            q = p_ref[sl, h * DK:(h + 1) * DK].astype(F32) * qscale
            k = p_ref[sl, 512 + h * DK:512 + (h + 1) * DK].astype(F32)
            v = p_ref[sl, 1024 + h * DV:1024 + (h + 1) * DV]
            bcum = _tri_cumsum(trig, log_alpha[sl, h * DK:(h + 1) * DK])
            qe = (q * jnp.exp(bcum)).astype(BF16)
            scores = jnp.where(causal_g, _dot_nt(qe, (k * jnp.exp(-bcum)).astype(BF16)), 0.0)
            ys.append(_dot(scores.astype(BF16), v) + _dot_nt(qe, st.astype(BF16)))
            b_tot = bcum[cg - 1:cg, :]
            kd = (k * jnp.exp(b_tot - bcum)).astype(BF16)
            st = jnp.exp(b_tot) * st + _dot_tn(v, kd)
        gs_ref[h] = st
        y = jnp.concatenate(ys, axis=0)
        yn = _head_norm(y, gng_ref[:, h * DV:(h + 1) * DV], False)
        gate = p_ref[:, 2048 + h * DV:2048 + (h + 1) * DV].astype(F32)
        o_ref[:, h * DV:(h + 1) * DV] = (yn * _silu(gate)).astype(o_ref.dtype)

    u = p_ref[:, 4096:6144].astype(F32)
    xbc = _silu(_causal_conv_tile(hist_ref, u, cw_ref, cb_ref, c))
    lane = lax.broadcasted_iota(jnp.int32, small.shape, 1)
    is_dt = (lane // GLA_RANK) == 1
    dt = jnp.where(is_dt, _softplus(small + dtb_ref[...]), 0.0)
    la = dt * -jnp.exp(alog_ref[...])
    cum = _tri_cumsum(tri_ref[...], la)
    pm = jnp.where(lane < GLA_RANK, _dt_shift(dt), cum)
    pmt = pm.T
    row = lax.broadcasted_iota(jnp.int32, (c, c), 0)
    col = lax.broadcasted_iota(jnp.int32, (c, c), 1)
    causal = row >= col
    for g in range(SSD_G):
        bm = xbc[:, 1024 + g * SSD_N:1024 + (g + 1) * SSD_N].astype(BF16)
        cm = xbc[:, 1536 + g * SSD_N:1536 + (g + 1) * SSD_N].astype(BF16)
        cb = _dot_nt(cm, bm)
        stg = ss_ref[g]
        y_inter = _dot(cm, stg.astype(BF16))
        y_parts, xs_parts, dec_parts = [], [], []
        for r in range(SSD_HPG):
            hd = g * SSD_HPG + r
            x_h = xbc[:, hd * SSD_P:(hd + 1) * SSD_P]
            dt_col = pm[:, hd:hd + 1]
            cum_col = pm[:, GLA_RANK + hd:GLA_RANK + hd + 1]
            cum_row = pmt[GLA_RANK + hd:GLA_RANK + hd + 1, :]
            seg = jnp.where(causal, cum_col - cum_row, NEG_BIG)
            w = cb * jnp.exp(seg)
            y_h = _dot(w.astype(BF16), (x_h * dt_col).astype(BF16))
            y_h = y_h + jnp.exp(cum_col) * y_inter[:, r * SSD_P:(r + 1) * SSD_P]
            y_parts.append(y_h + dskip_ref[:, hd * SSD_P:(hd + 1) * SSD_P] * x_h)
            last = cum_col[c - 1:c, :]
            xs_parts.append(x_h * (dt_col * jnp.exp(last - cum_col)))
            dec_parts.append(jnp.broadcast_to(jnp.exp(last), (1, SSD_P)))
        xs = jnp.concatenate(xs_parts, axis=1).astype(BF16)
        dec = jnp.concatenate(dec_parts, axis=1)
        ss_ref[g] = dec * stg + _dot_tn(bm, xs)
        yg = jnp.concatenate(y_parts, axis=1) * _silu(p_ref[:, 3072 + g * DV:3072 + (g + 1) * DV].astype(F32))
        o_ref[:, 1024 + g * DV:1024 + (g + 1) * DV] = _head_norm(
            yg, sng_ref[:, g * DV:(g + 1) * DV], False).astype(o_ref.dtype)


def _dt_shift(dt):
    return pltpu.roll(dt, LANES - GLA_RANK, 1)


def _mixer_cd(proj, small, tri, trig, wal, bal, gla_g, conv_w, conv_b, dtb, alog, dskip, ssd_g, b, s, c, cg):
    nt = s // c
    t = b * s
    tok = lambda bi, i: (bi * nt + i, 0)
    full2 = lambda bi, i: (0, 0)
    ch = D_MODEL + 2 * SSD_G * SSD_N
    return pl.pallas_call(
        functools.partial(_mixer_cd_kernel, c=c, cg=cg),
        grid=(b, nt),
        in_specs=[
            pl.BlockSpec((c, P_MAIN), tok),
            pl.BlockSpec((c, LANES), tok),
            pl.BlockSpec((c, c), full2),
            pl.BlockSpec((cg, cg), full2),
            pl.BlockSpec((LANES, N_HEADS * DK), full2),
            pl.BlockSpec((1, N_HEADS * DK), full2),
            pl.BlockSpec((1, D_MODEL), full2),
            pl.BlockSpec((CONV_K, ch), full2),
            pl.BlockSpec((1, ch), full2),
            pl.BlockSpec((1, LANES), full2),
            pl.BlockSpec((1, LANES), full2),
            pl.BlockSpec((1, D_MODEL), full2),
            pl.BlockSpec((1, D_MODEL), full2),
        ],
        out_specs=pl.BlockSpec((c, 2 * D_MODEL), tok),
        out_shape=jax.ShapeDtypeStruct((t, 2 * D_MODEL), BF16),
        scratch_shapes=[
            pltpu.VMEM((N_HEADS, DV, DK), F32),
            pltpu.VMEM((SSD_G, SSD_N, SSD_HPG * SSD_P), F32),
            pltpu.VMEM((SUBLANES + c, ch), F32),
        ],
        compiler_params=_cparams(("parallel", "arbitrary")),
        name="mixer_cd",
    )(proj, small, tri, trig, wal, bal, gla_g, conv_w, conv_b, dtb, alog, dskip, ssd_g)


def _layernorm(z, g, b):
    zc = z - jnp.mean(z, axis=-1, keepdims=True)
    return zc * lax.rsqrt(jnp.mean(zc * zc, axis=-1, keepdims=True) + LN_EPS) * g + b


def _outproj_ln_kernel(y_ref, w_ref, h_ref, g_ref, b_ref, o_ref):
    z = ALPHA * h_ref[...] + _dot(y_ref[...], w_ref[...])
    o_ref[...] = _layernorm(z, g_ref[...], b_ref[...])


def _outproj_ln(y, w, h, g, b, tm=512):
    t, kdim = y.shape
    tm = min(tm, t)
    d = h.shape[1]
    return pl.pallas_call(
        _outproj_ln_kernel,
        grid=(t // tm,),
        in_specs=[
            pl.BlockSpec((tm, kdim), lambda i: (i, 0)),
            pl.BlockSpec((kdim, d), lambda i: (0, 0)),
            pl.BlockSpec((tm, d), lambda i: (i, 0)),
            pl.BlockSpec((1, d), lambda i: (0, 0)),
            pl.BlockSpec((1, d), lambda i: (0, 0)),
        ],
        out_specs=pl.BlockSpec((tm, d), lambda i: (i, 0)),
        out_shape=jax.ShapeDtypeStruct((t, d), F32),
        compiler_params=_cparams(("parallel",)),
        name="outproj_ln",
    )(y, w, h, g, b)


def _split2(x):
    hi = x.astype(BF16)
    return hi, (x - hi.astype(F32)).astype(BF16)


def _router_kernel(h_ref, w_ref, b_ref, ltri_ref, utri_ref, pos_ref, gate_ref, tot_ref, carry_ref, *, cap):
    @pl.when(pl.program_id(0) == 0)
    def _():
        carry_ref[...] = jnp.zeros_like(carry_ref)

    h_hi, h_lo = _split2(h_ref[...])
    w_hi, w_lo = _split2(w_ref[...])
    logits = (_dot(h_hi, w_hi) + _dot(h_lo, w_hi) + _dot(h_hi, w_lo)).T[:N_EXPERTS, :]
    e = jnp.exp(logits - jnp.max(logits, axis=0, keepdims=True))
    probs = e / jnp.sum(e, axis=0, keepdims=True)
    sel = probs + b_ref[...]
    row = lax.broadcasted_iota(jnp.int32, sel.shape, 0)
    pos = row & (EPG - 1)
    gidx = row // EPG

    def member(a, k):
        fwd = pltpu.roll(a, N_EXPERTS - k, 0)
        back = pltpu.roll(a, EPG - k, 0)
        return jnp.where(pos + k < EPG, fwd, back)

    others = [member(sel, k) for k in range(1, EPG)]
    vals = [sel] + others
    top2 = None
    for i in range(EPG):
        for j in range(i + 1, EPG):
            pair = vals[i] + vals[j]
            top2 = pair if top2 is None else jnp.maximum(top2, pair)
    best = jnp.ones(sel.shape, jnp.int32)
    for k in range(1, N_GROUPS):
        other = pltpu.roll(top2, EPG * k, 0)
        wins = jnp.where(gidx >= k, jnp.where(top2 > other, 1, 0), jnp.where(top2 >= other, 1, 0))
        best = best * wins
    rank = jnp.zeros(sel.shape, jnp.int32)
    for k in range(1, EPG):
        v = others[k - 1]
        tie = jnp.where(pos + k >= EPG, 1, 0)
        rank = rank + jnp.where(v > sel, 1, jnp.where(v == sel, tie, 0))
    chosen = jnp.where(best * jnp.where(rank < 2, 1, 0) > 0, 1.0, 0.0)
    gsel = chosen * probs
    comb = gsel / jnp.sum(gsel, axis=0, keepdims=True)

    chosen_b = chosen.astype(BF16)
    lower = _dot(ltri_ref[...], chosen_b)
    first = chosen * jnp.where(lower == 0.0, 1.0, 0.0)
    second = chosen - first
    carry = carry_ref[:, 0:1]
    before = _dot(chosen_b, utri_ref[...])
    slot = carry + before + row.astype(F32) * float(cap)
    pos0 = jnp.sum(first * slot, axis=0, keepdims=True)
    pos1 = jnp.sum(second * slot, axis=0, keepdims=True)
    g0 = jnp.sum(first * comb, axis=0, keepdims=True)
    g1 = jnp.sum(second * comb, axis=0, keepdims=True)
    r8 = lax.broadcasted_iota(jnp.int32, (SUBLANES, sel.shape[1]), 0)
    pos_ref[...] = jnp.where(r8 == 0, pos0, jnp.where(r8 == 1, pos1, 0.0)).astype(jnp.int32)
    gate_ref[...] = jnp.where(r8 == 0, g0, jnp.where(r8 == 1, g1, 0.0))
    total = carry_ref[...] + jnp.sum(chosen, axis=1, keepdims=True)
    carry_ref[...] = total
    tot_ref[...] = total.astype(jnp.int32)


def _router(h, w_pad, b_col, cap, tm=1024):
    t, d = h.shape
    tm = min(tm, t)
    ltri = jnp.asarray(np.tril(np.ones((N_EXPERTS, N_EXPERTS), np.float32), -1), BF16)
    utri = jnp.asarray(np.triu(np.ones((tm, tm), np.float32), 1), BF16)
    return pl.pallas_call(
        functools.partial(_router_kernel, cap=cap),
        grid=(t // tm,),
        in_specs=[
            pl.BlockSpec((tm, d), lambda i: (i, 0)),
            pl.BlockSpec((d, LANES), lambda i: (0, 0)),
            pl.BlockSpec((N_EXPERTS, 1), lambda i: (0, 0)),
            pl.BlockSpec((N_EXPERTS, N_EXPERTS), lambda i: (0, 0)),
            pl.BlockSpec((tm, tm), lambda i: (0, 0)),
        ],
        out_specs=[
            pl.BlockSpec((SUBLANES, tm), lambda i: (0, i)),
            pl.BlockSpec((SUBLANES, tm), lambda i: (0, i)),
            pl.BlockSpec((N_EXPERTS, LANES), lambda i: (0, 0)),
        ],
        out_shape=[
            jax.ShapeDtypeStruct((SUBLANES, t), jnp.int32),
            jax.ShapeDtypeStruct((SUBLANES, t), F32),
            jax.ShapeDtypeStruct((N_EXPERTS, LANES), jnp.int32),
        ],
        scratch_shapes=[pltpu.VMEM((N_EXPERTS, LANES), F32)],
        compiler_params=_cparams(("arbitrary",)),
        name="router",
    )(h, w_pad, b_col, ltri, utri)


def _rebase_kernel(base_ref, pos_ref, o_ref, *, cap):
    p = pos_ref[...]
    e = p // cap
    out = p - e * cap
    for k in range(N_EXPERTS):
        out = out + jnp.where(e == k, base_ref[k], 0)
    o_ref[...] = out


def _rebase(pos, base, cap, tm=8192):
    t = pos.shape[1]
    tm = min(tm, t)
    return pl.pallas_call(
        functools.partial(_rebase_kernel, cap=cap),
        grid_spec=pltpu.PrefetchScalarGridSpec(
            num_scalar_prefetch=1,
            grid=(t // tm,),
            in_specs=[pl.BlockSpec((SUBLANES, tm), lambda i, base: (0, i))],
            out_specs=pl.BlockSpec((SUBLANES, tm), lambda i, base: (0, i)),
        ),
        out_shape=jax.ShapeDtypeStruct(pos.shape, jnp.int32),
        compiler_params=_cparams(("arbitrary",)),
        name="rebase",
    )(base, pos)


def _dispatch_kernel(lo_ref, hi_ref, pos_ref, h_ref, xs_ref, sem, *, tm):
    def issue(r, carry):
        src = h_ref.at[pl.ds(r, 1), :]
        pltpu.make_async_copy(src, xs_ref.at[pl.ds(pos_ref[0, r], 1), :], sem).start()
        pltpu.make_async_copy(src, xs_ref.at[pl.ds(pos_ref[1, r], 1), :], sem).start()
        return carry

    lax.fori_loop(0, tm, issue, 0, unroll=8)

    @pl.when(pl.program_id(0) == pl.num_programs(0) - 1)
    def _():
        src = h_ref.at[pl.ds(0, 1), :]

        def fill(r, carry):
            pltpu.make_async_copy(src, xs_ref.at[pl.ds(r, 1), :], sem).start()
            return carry

        def drain(r, carry):
            pltpu.make_async_copy(src, xs_ref.at[pl.ds(0, 1), :], sem).wait()
            return carry

        for k in range(N_EXPERTS + 1):
            lax.fori_loop(lo_ref[k], hi_ref[k], fill, 0)
            lax.fori_loop(lo_ref[k], hi_ref[k], drain, 0)

    whole = pltpu.make_async_copy(h_ref, xs_ref.at[pl.ds(0, tm), :], sem)
    whole.wait()
    whole.wait()


def _dispatch(h, pos, fill_lo, fill_hi, n_rows, tm=512):
    t, d = h.shape
    tm = min(tm, t)
    return pl.pallas_call(
        functools.partial(_dispatch_kernel, tm=tm),
        grid_spec=pltpu.PrefetchScalarGridSpec(
            num_scalar_prefetch=2,
            grid=(t // tm,),
            in_specs=[
                pl.BlockSpec((SUBLANES, tm), lambda i, lo, hi: (0, i), memory_space=pltpu.SMEM),
                pl.BlockSpec((tm, d), lambda i, lo, hi: (i, 0)),
            ],
            out_specs=pl.BlockSpec(memory_space=pl.ANY),
            scratch_shapes=[pltpu.SemaphoreType.DMA],
        ),
        out_shape=jax.ShapeDtypeStruct((n_rows, d), F32),
        compiler_params=_cparams(("arbitrary",)),
        name="dispatch",
    )(fill_lo, fill_hi, pos, h)


def _ffn_kernel(exp_ref, x_ref, wg_ref, wu_ref, wd_ref, y_ref):
    xb = x_ref[...].astype(BF16)
    act = _silu(_dot(xb, wg_ref[...])) * _dot(xb, wu_ref[...])
    y_ref[...] = _dot(act.astype(BF16), wd_ref[...])


def _ffn(xs, tile_exp, wg, wu, wd):
    n, d = xs.shape
    return pl.pallas_call(
        _ffn_kernel,
        grid_spec=pltpu.PrefetchScalarGridSpec(
            num_scalar_prefetch=1,
            grid=(n // TM_F,),
            in_specs=[
                pl.BlockSpec((TM_F, d), lambda i, ex: (i, 0)),
                pl.BlockSpec((None, d, D_FF), lambda i, ex: (ex[i], 0, 0)),
                pl.BlockSpec((None, d, D_FF), lambda i, ex: (ex[i], 0, 0)),
                pl.BlockSpec((None, D_FF, d), lambda i, ex: (ex[i], 0, 0)),
            ],
            out_specs=pl.BlockSpec((TM_F, d), lambda i, ex: (i, 0)),
        ),
        out_shape=jax.ShapeDtypeStruct((n, d), F32),
        compiler_params=_cparams(("arbitrary",)),
        name="expert_ffn",
    )(tile_exp, xs, wg, wu, wd)


def _combine_kernel(pos_ref, gate_ref, h_ref, ys_ref, g_ref, b_ref, o_ref, buf_ref, sem, *, tm):
    def issue(r, carry):
        pltpu.make_async_copy(ys_ref.at[pl.ds(pos_ref[0, r], 1), :], buf_ref.at[0, pl.ds(r, 1), :], sem).start()
        pltpu.make_async_copy(ys_ref.at[pl.ds(pos_ref[1, r], 1), :], buf_ref.at[1, pl.ds(r, 1), :], sem).start()
        return carry

    lax.fori_loop(0, tm, issue, 0, unroll=8)
    gt = gate_ref[...].T
    for k in range(2):
        pltpu.make_async_copy(ys_ref.at[pl.ds(0, tm), :], buf_ref.at[k], sem).wait()
    z = ALPHA * h_ref[...] + gt[:, 0:1] * buf_ref[0] + gt[:, 1:2] * buf_ref[1]
    o_ref[...] = _layernorm(z, g_ref[...], b_ref[...])


def _combine(h, pos, gates, ys, g, b, tm=512):
    t, d = h.shape
    tm = min(tm, t)
    return pl.pallas_call(
        functools.partial(_combine_kernel, tm=tm),
        grid=(t // tm,),
        in_specs=[
            pl.BlockSpec((SUBLANES, tm), lambda i: (0, i), memory_space=pltpu.SMEM),
            pl.BlockSpec((SUBLANES, tm), lambda i: (0, i)),
            pl.BlockSpec((tm, d), lambda i: (i, 0)),
            pl.BlockSpec(memory_space=pl.ANY),
            pl.BlockSpec((1, d), lambda i: (0, 0)),
            pl.BlockSpec((1, d), lambda i: (0, 0)),
        ],
        out_specs=pl.BlockSpec((tm, d), lambda i: (i, 0)),
        out_shape=jax.ShapeDtypeStruct((t, d), F32),
        scratch_shapes=[pltpu.VMEM((2, tm, d), F32), pltpu.SemaphoreType.DMA],
        compiler_params=_cparams(("arbitrary",)),
        name="combine_ln",
    )(pos, gates, h, ys, g, b)


def _moe_ln(h, w_pad, b_col, wg, wu, wd, g, b):
    t, d = h.shape
    cap = 2 * t
    nt = (2 * t) // TM_F + N_EXPERTS
    pos, gates, tot = _router(h, w_pad, b_col, cap)
    totals = tot[:, 0]
    ntile = (totals + TM_F - 1) // TM_F
    ends = jnp.cumsum(ntile)
    base = (ends - ntile) * TM_F
    tile_exp = jnp.minimum(jnp.sum((jnp.arange(nt, dtype=jnp.int32)[:, None] >= ends[None, :]).astype(jnp.int32),
                                   axis=1), N_EXPERTS - 1)
    fill_lo = jnp.concatenate([base + totals, ends[-1:] * TM_F]).astype(jnp.int32)
    fill_hi = jnp.concatenate([ends * TM_F, jnp.full((1,), nt * TM_F, jnp.int32)]).astype(jnp.int32)
    pos = _rebase(pos, base.astype(jnp.int32), cap)
    xs = _dispatch(h, pos, fill_lo, fill_hi, nt * TM_F)
    ys = _ffn(xs, tile_exp.astype(jnp.int32), wg, wu, wd)
    return _combine(h, pos, gates, ys, g, b)


def _retention_tables(c):
    log_g = np.log(1.0 - 2.0 ** (-5.0 - np.arange(N_HEADS, dtype=np.float32))).astype(np.float32)
    idx = np.arange(c, dtype=np.float32)
    diff = idx[:, None] - idx[None, :]
    dmask = np.where(diff >= 0, np.exp(log_g[:, None, None] * np.maximum(diff, 0.0)), 0.0)
    qdec = np.exp(log_g[:, None] * (idx + 1.0))[..., None] * np.ones((1, 1, DK), np.float32)
    kdec = np.exp(log_g[:, None] * (c - 1.0 - idx))[..., None] * np.ones((1, 1, DK), np.float32)
    cdec = np.exp(log_g * c)[:, None, None] * np.ones((1, 1, DV), np.float32)
    return tuple(jnp.asarray(a, F32) for a in (dmask, qdec, kdec, cdec))


def _tri(c):
    return jnp.asarray(np.tril(np.ones((c, c), np.float32)), BF16)


def _pad_cols(w, n):
    return jnp.pad(w, ((0, 0), (0, n - w.shape[1])))


def kernel(x, positions, w_in_ab, ret_norm_g, ml_conv_w, ml_conv_b, ml_b_i, ml_b_f, ml_norm_g, w_out_ab,
           w_in_cd, gla_w_alpha, gla_b_alpha, gla_norm_g, ssd_conv_w, ssd_conv_b, ssd_dt_bias, ssd_a_log, ssd_d,
           ssd_norm_g, w_out_cd, w_router, b_router, moe_w_gate, moe_w_up, moe_w_down,
           ln_mix_g, ln_mix_b, ln_ffn_g, ln_ffn_b):
    b, s, d = x.shape
    t = b * s
    c = min(256, s)
    cg = min(64, c)
    h = x.reshape(t, d)
    row = lambda v: v.reshape(1, -1).astype(F32)

    w_pad = _pad_cols(w_router.astype(F32), LANES)
    b_col = b_router.astype(F32).reshape(N_EXPERTS, 1)
    wg = moe_w_gate.astype(BF16)
    wu = moe_w_up.astype(BF16)
    wd = moe_w_down.astype(BF16)

    def ffn(hh, layer):
        return _moe_ln(hh, w_pad, b_col, wg[layer], wu[layer], wd[layer], row(ln_ffn_g[layer]), row(ln_ffn_b[layer]))

    half = np.concatenate([np.arange(0, DK, 2), np.arange(1, DK, 2)])
    perm = np.concatenate([hd * DK + half for hd in range(N_HEADS)])
    w = w_in_ab[0]
    nqk = N_HEADS * DK
    w_main = jnp.concatenate([w[:, :nqk][:, perm], w[:, nqk:2 * nqk][:, perm], w[:, 2 * nqk:P_MAIN]], axis=1)
    proj, gates = _inproj(h, w_main.astype(BF16), _pad_cols(w[:, P_MAIN:], LANES).astype(BF16))
    inv = ROPE_BASE ** (-jnp.arange(0, DK, 2, dtype=F32) / DK)
    cc, ss = _rope_tables(positions.reshape(t, 1), jnp.concatenate([inv, inv]).reshape(1, DK))
    gate_b = _pad_cols(jnp.concatenate([ml_b_i[0], ml_b_f[0]]).reshape(1, -1).astype(F32), LANES)
    y = _mixer_ab(proj, gates, cc, ss, _retention_tables(c) + (_tri(c),), row(ret_norm_g[0]), row(ml_norm_g[0]),
                  ml_conv_w[0].astype(F32), row(ml_conv_b[0]), gate_b, b, s, c)
    h = _outproj_ln(y, w_out_ab[0].astype(BF16), h, row(ln_mix_g[0]), row(ln_mix_b[0]))
    h = ffn(h, 0)

    w = w_in_cd[0]
    g0 = 2 * nqk + 2 * N_HEADS * DV
    s0 = g0 + GLA_RANK
    s1 = s0 + D_MODEL + D_MODEL + 2 * SSD_G * SSD_N
    w_main = jnp.concatenate([w[:, :g0], w[:, s0:s1]], axis=1)
    w_small = _pad_cols(jnp.concatenate([w[:, g0:s0], w[:, s1:]], axis=1), LANES)
    proj, small = _inproj(h, w_main.astype(BF16), w_small.astype(BF16))
    wal = jnp.pad(gla_w_alpha[0], ((0, LANES - GLA_RANK), (0, 0))).astype(BF16)
    lane_pad = lambda v: jnp.pad(v.reshape(1, -1).astype(F32), ((0, 0), (GLA_RANK, LANES - GLA_RANK - SSD_H)))
    dskip = jnp.repeat(ssd_d[0].astype(F32), SSD_P).reshape(1, -1)
    y = _mixer_cd(proj, small, _tri(c), _tri(cg), wal, row(gla_b_alpha[0]), row(gla_norm_g[0]),
                  ssd_conv_w[0].astype(F32), row(ssd_conv_b[0]), lane_pad(ssd_dt_bias[0]), lane_pad(ssd_a_log[0]),
                  dskip, row(ssd_norm_g[0]), b, s, c, cg)
    h = _outproj_ln(y, w_out_cd[0].astype(BF16), h, row(ln_mix_g[1]), row(ln_mix_b[1]))
    h = ffn(h, 1)
    return h.reshape(b, s, d)
```

```python
import functools
import math

import jax
import jax.numpy as jnp
import numpy as np
from jax import lax
from jax.experimental import pallas as pl
from jax.experimental.pallas import tpu as pltpu

F32 = jnp.float32
BF16 = jnp.bfloat16

D_MODEL = 1024
DEPTH = 2
LN_EPS = 1e-5
NORM_EPS = 1e-6
N_HEADS = 4
DK = D_MODEL // 8
DV = D_MODEL // 4
ROPE_BASE = 10000.0
CONV_K = 4
GLA_RANK = 16
GLA_TAU = 16.0
SSD_P = 64
SSD_H = D_MODEL // SSD_P
SSD_G = 4
SSD_HPG = SSD_H // SSD_G
SSD_N = 128
N_EXPERTS = 16
N_GROUPS = 4
EPG = N_EXPERTS // N_GROUPS
D_FF = D_MODEL // 2
ALPHA = (2.0 * DEPTH) ** 0.25

LANES = 128
SUBLANES = 8
VMEM_LIMIT = 56 * 1024 * 1024

P_MAIN = 6 * D_MODEL
TM_F = 512
NEG_BIG = -1e30


def _cparams(sem):
    return pltpu.CompilerParams(dimension_semantics=sem, vmem_limit_bytes=VMEM_LIMIT)


def _dot(a, b):
    return jnp.dot(a, b, preferred_element_type=F32)


def _dot_nt(a, b):
    return lax.dot_general(a, b, (((1,), (1,)), ((), ())), preferred_element_type=F32)


def _dot_tn(a, b):
    return lax.dot_general(a, b, (((0,), (0,)), ((), ())), preferred_element_type=F32)


def _sigmoid(x):
    return 1.0 / (1.0 + jnp.exp(-x))


def _silu(x):
    return x * _sigmoid(x)


def _log_sigmoid(x):
    return jnp.minimum(x, 0.0) - jnp.log(1.0 + jnp.exp(-jnp.abs(x)))


def _softplus(x):
    return jnp.maximum(x, 0.0) + jnp.log(1.0 + jnp.exp(-jnp.abs(x)))


def _tri_cumsum(tri, x):
    hi = x.astype(BF16)
    r1 = x - hi.astype(F32)
    mid = r1.astype(BF16)
    lo = (r1 - mid.astype(F32)).astype(BF16)
    return _dot(tri, hi) + _dot(tri, mid) + _dot(tri, lo)


def _head_norm(y, g, center):
    if center:
        y = y - jnp.mean(y, axis=-1, keepdims=True)
    return y * lax.rsqrt(jnp.mean(y * y, axis=-1, keepdims=True) + NORM_EPS) * g


def _in_projection(x_ref, wm_ref, ws_ref, p_ref, s_ref):
    xb = x_ref[...].astype(BF16)
    for j in range(P_MAIN // D_MODEL):
        sl = slice(j * D_MODEL, (j + 1) * D_MODEL)
        p_ref[:, sl] = _dot(xb, wm_ref[:, sl]).astype(BF16)
    s_ref[...] = _dot(xb, ws_ref[...])


def _pipelined_in_projection(x_ref, xn_ref, wm_ref, ws_ref, pbuf_ref, sbuf_ref, xb_ref):
    step = pl.program_id(0)
    slot = step % 2

    @pl.when(step == 0)
    def _():
        _in_projection(x_ref, wm_ref, ws_ref, pbuf_ref.at[0], sbuf_ref.at[0])

    nxt_p = pbuf_ref.at[1 - slot]
    nxt_s = sbuf_ref.at[1 - slot]
    xb_ref[...] = xn_ref[...].astype(BF16)

    def main_piece(j):
        def run():
            sl = slice(j * D_MODEL, (j + 1) * D_MODEL)
            nxt_p[:, sl] = _dot(xb_ref[...], wm_ref[:, sl]).astype(BF16)
        return run

    def small_piece():
        nxt_s[...] = _dot(xb_ref[...], ws_ref[...])

    thunks = [main_piece(j) for j in range(P_MAIN // D_MODEL)] + [small_piece]
    return pbuf_ref.at[slot], sbuf_ref[slot], thunks


def _layernorm(z, g, b):
    zc = z - jnp.mean(z, axis=-1, keepdims=True)
    return zc * lax.rsqrt(jnp.mean(zc * zc, axis=-1, keepdims=True) + LN_EPS) * g + b


def _rope_kernel(pos_ref, inv_ref, cc_ref, ss_ref):
    ang = pos_ref[...].astype(F32) * inv_ref[...]
    lane = lax.broadcasted_iota(jnp.int32, ang.shape, 1)
    s = jnp.sin(ang)
    cc_ref[...] = jnp.cos(ang)
    ss_ref[...] = jnp.where(lane < DK // 2, -s, s)


def _rope_tables(pos_col, inv2, tm=2048):
    t = pos_col.shape[0]
    tm = min(tm, t)
    return pl.pallas_call(
        _rope_kernel,
        grid=(t // tm,),
        in_specs=[pl.BlockSpec((tm, 1), lambda i: (i, 0)), pl.BlockSpec((1, DK), lambda i: (0, 0))],
        out_specs=[pl.BlockSpec((tm, DK), lambda i: (i, 0))] * 2,
        out_shape=[jax.ShapeDtypeStruct((t, DK), F32)] * 2,
        compiler_params=_cparams(("parallel",)),
        name="rope_tables",
    )(pos_col, inv2)


def _causal_conv_tile(hist_ref, u, w_ref, b_ref, c):
    hist_ref[pl.ds(SUBLANES, c), :] = u
    acc = b_ref[...] + w_ref[CONV_K - 1:CONV_K, :] * u
    for j in range(CONV_K - 1):
        acc = acc + w_ref[j:j + 1, :] * hist_ref[pl.ds(SUBLANES - (CONV_K - 1) + j, c), :]
    hist_ref[pl.ds(0, SUBLANES), :] = hist_ref[pl.ds(c, SUBLANES), :]
    return acc


def _layer_ab_kernel(x_ref, xn_ref, wm_ref, ws_ref, cc_ref, ss_ref, dmask_ref, qdec_ref, kdec_ref, cdec_ref,
                     tri_ref, rng_ref, mng_ref, cw_ref, cb_ref, gb_ref, wo_ref, lg_ref, lb_ref, o_ref,
                     pbuf_ref, sbuf_ref, xb_ref, z_ref, rs_ref, mc_ref, mn_ref, mm_ref, hist_ref, *, c, nt):
    @pl.when(pl.program_id(0) % nt == 0)
    def _():
        rs_ref[...] = jnp.zeros_like(rs_ref)
        mc_ref[...] = jnp.zeros_like(mc_ref)
        mn_ref[...] = jnp.zeros_like(mn_ref)
        mm_ref[...] = jnp.zeros_like(mm_ref)
        hist_ref[pl.ds(0, SUBLANES), :] = jnp.zeros((SUBLANES, hist_ref.shape[1]), F32)

    p_ref, small, next_proj = _pipelined_in_projection(x_ref, xn_ref, wm_ref, ws_ref, pbuf_ref, sbuf_ref, xb_ref)
    z_ref[...] = ALPHA * x_ref[...]
    cc = cc_ref[...]
    ss = ss_ref[...]
    kscale = DK ** -0.5

    def rot(t):
        return t * cc + pltpu.roll(t, DK // 2, 1) * ss

    heads = range(N_HEADS)
    qs = [rot(p_ref[:, h * DK:(h + 1) * DK].astype(F32)) for h in heads]
    ks = [rot(p_ref[:, 512 + h * DK:512 + (h + 1) * DK].astype(F32)) * kscale for h in heads]
    vs = [p_ref[:, 1024 + h * DV:1024 + (h + 1) * DV] for h in heads]
    next_proj[0]()
    scs = [_dot_nt(qs[h].astype(BF16), ks[h].astype(BF16)) * dmask_ref[h] for h in heads]
    next_proj[1]()
    ys = [_dot(scs[h].astype(BF16), vs[h]) + _dot((qs[h] * qdec_ref[h]).astype(BF16), rs_ref[h].astype(BF16))
          for h in heads]
    next_proj[2]()
    for h in heads:
        rs_ref[h] = cdec_ref[h] * rs_ref[h] + _dot_tn((ks[h] * kdec_ref[h]).astype(BF16), vs[h])
    yns = [_head_norm(ys[h], rng_ref[:, h * DV:(h + 1) * DV], True)
           * _silu(p_ref[:, 2048 + h * DV:2048 + (h + 1) * DV].astype(F32)) for h in heads]
    next_proj[3]()
    for h in heads:
        z_ref[...] += _dot(yns[h].astype(BF16), wo_ref[h * DV:(h + 1) * DV, :])

    u = p_ref[:, 3072:4096].astype(F32)
    qk = _silu(_causal_conv_tile(hist_ref, u, cw_ref, cb_ref, c))
    gates = small + gb_ref[...]
    lane = lax.broadcasted_iota(jnp.int32, gates.shape, 1)
    is_f = (lane // N_HEADS) == 1
    logf = jnp.where(is_f, _log_sigmoid(gates), 0.0)
    bcum = _tri_cumsum(tri_ref[...], logf)
    pm = jnp.where(lane < N_HEADS, gates, bcum)
    pmt = pm.T
    row = lax.broadcasted_iota(jnp.int32, (c, c), 0)
    col = lax.broadcasted_iota(jnp.int32, (c, c), 1)
    causal = row >= col
    qs = [qk[:, h * DK:(h + 1) * DK] for h in heads]
    ks = [qk[:, 512 + h * DK:512 + (h + 1) * DK] * kscale for h in heads]
    vs = [p_ref[:, 4096 + h * DV:4096 + (h + 1) * DV] for h in heads]
    i_cols = [pm[:, h:h + 1] for h in heads]
    b_cols = [pm[:, N_HEADS + h:N_HEADS + h + 1] for h in heads]
    m_prevs = [mm_ref[h][:, 0:1] for h in heads]
    a_s = [b_cols[h] + m_prevs[h] for h in heads]
    ds = [jnp.where(causal, b_cols[h] - pmt[N_HEADS + h:N_HEADS + h + 1, :] + pmt[h:h + 1, :], NEG_BIG)
          for h in heads]
    m_ts = [jnp.maximum(a_s[h], jnp.max(ds[h], axis=1, keepdims=True)) for h in heads]
    w_inters = [jnp.exp(a_s[h] - m_ts[h]) for h in heads]
    next_proj[4]()
    s_qks = [_dot_nt(qs[h].astype(BF16), ks[h].astype(BF16)) * jnp.exp(ds[h] - m_ts[h]) for h in heads]
    next_proj[5]()
    nums = [_dot(s_qks[h].astype(BF16), vs[h]) + w_inters[h] * _dot(qs[h].astype(BF16), mc_ref[h].astype(BF16))
            for h in heads]
    dens = [jnp.sum(s_qks[h], axis=1, keepdims=True)
            + w_inters[h] * jnp.sum(qs[h] * mn_ref[h], axis=1, keepdims=True) for h in heads]
    hhs = [nums[h] / jnp.maximum(jnp.abs(dens[h]), jnp.exp(-m_ts[h])) for h in heads]
    next_proj[6]()
    for h in heads:
        b_tot = b_cols[h][c - 1:c, :]
        lw = b_tot - b_cols[h] + i_cols[h]
        m_new = jnp.maximum(b_tot + m_prevs[h], jnp.max(lw, axis=0, keepdims=True))
        kd = ks[h] * jnp.exp(lw - m_new)
        decay = jnp.exp(b_tot + m_prevs[h] - m_new)
        mc_ref[h] = decay * mc_ref[h] + _dot_tn(kd.astype(BF16), vs[h])
        mn_ref[h] = decay * mn_ref[h] + jnp.sum(kd, axis=0, keepdims=True)
        mm_ref[h] = jnp.broadcast_to(m_new, (1, LANES))
    yns = [_head_norm(hhs[h] * _sigmoid(p_ref[:, 5120 + h * DV:5120 + (h + 1) * DV].astype(F32)),
                      mng_ref[:, h * DV:(h + 1) * DV], True) for h in heads]
    for h in heads:
        z_ref[...] += _dot(yns[h].astype(BF16), wo_ref[D_MODEL + h * DV:D_MODEL + (h + 1) * DV, :])

    o_ref[...] = _layernorm(z_ref[...], lg_ref[...], lb_ref[...])


def _layer_ab(x, w_main, w_small, cc, ss, tabs, ret_g, ml_g, conv_w, conv_b, gate_b, w_out, ln_g, ln_b, b, s, c):
    nt = s // c
    t, d = x.shape
    n = b * nt
    tok = lambda i: (i, 0)
    nxt = lambda i: (jnp.minimum(i + 1, n - 1), 0)
    full2 = lambda i: (0, 0)
    full3 = lambda i: (0, 0, 0)
    dmask, qdec, kdec, cdec, tri = tabs
    return pl.pallas_call(
        functools.partial(_layer_ab_kernel, c=c, nt=nt),
        grid=(n,),
        in_specs=[
            pl.BlockSpec((c, d), tok),
            pl.BlockSpec((c, d), nxt),
            pl.BlockSpec((d, P_MAIN), full2),
            pl.BlockSpec((d, LANES), full2),
            pl.BlockSpec((c, DK), tok),
            pl.BlockSpec((c, DK), tok),
            pl.BlockSpec((N_HEADS, c, c), full3),
            pl.BlockSpec((N_HEADS, c, DK), full3),
            pl.BlockSpec((N_HEADS, c, DK), full3),
            pl.BlockSpec((N_HEADS, 1, DV), full3),
            pl.BlockSpec((c, c), full2),
            pl.BlockSpec((1, D_MODEL), full2),
            pl.BlockSpec((1, D_MODEL), full2),
            pl.BlockSpec((CONV_K, D_MODEL), full2),
            pl.BlockSpec((1, D_MODEL), full2),
            pl.BlockSpec((1, LANES), full2),
            pl.BlockSpec((2 * D_MODEL, d), full2),
            pl.BlockSpec((1, d), full2),
            pl.BlockSpec((1, d), full2),
        ],
        out_specs=pl.BlockSpec((c, d), tok),
        out_shape=jax.ShapeDtypeStruct((t, d), F32),
        scratch_shapes=[
            pltpu.VMEM((2, c, P_MAIN), BF16),
            pltpu.VMEM((2, c, LANES), F32),
            pltpu.VMEM((c, d), BF16),
            pltpu.VMEM((c, d), F32),
            pltpu.VMEM((N_HEADS, DK, DV), F32),
            pltpu.VMEM((N_HEADS, DK, DV), F32),
            pltpu.VMEM((N_HEADS, 1, DK), F32),
            pltpu.VMEM((N_HEADS, 1, LANES), F32),
            pltpu.VMEM((SUBLANES + c, D_MODEL), F32),
        ],
        compiler_params=_cparams(("arbitrary",)),
        name="layer_ab",
    )(x, x, w_main, w_small, cc, ss, dmask, qdec, kdec, cdec, tri, ret_g, ml_g, conv_w, conv_b, gate_b,
      w_out, ln_g, ln_b)


def _layer_cd_kernel(x_ref, xn_ref, wm_ref, ws_ref, tri_ref, trig_ref, wal_ref, bal_ref, gng_ref, cw_ref, cb_ref,
                     dtb_ref, alog_ref, dskip_ref, sng_ref, wo_ref, lg_ref, lb_ref, o_ref,
                     pbuf_ref, sbuf_ref, xb_ref, z_ref, gs_ref, ss_ref, hist_ref, *, c, cg, nt):
    @pl.when(pl.program_id(0) % nt == 0)
    def _():
        gs_ref[...] = jnp.zeros_like(gs_ref)
        ss_ref[...] = jnp.zeros_like(ss_ref)
        hist_ref[pl.ds(0, SUBLANES), :] = jnp.zeros((SUBLANES, hist_ref.shape[1]), F32)

    p_ref, small, next_proj = _pipelined_in_projection(x_ref, xn_ref, wm_ref, ws_ref, pbuf_ref, sbuf_ref, xb_ref)
    z_ref[...] = ALPHA * x_ref[...]

    log_alpha = _log_sigmoid(_dot(small.astype(BF16), wal_ref[...]) + bal_ref[...]) * (1.0 / GLA_TAU)
    qscale = DK ** -0.5
    trig = trig_ref[...]
    rowg = lax.broadcasted_iota(jnp.int32, (cg, cg), 0)
    colg = lax.broadcasted_iota(jnp.int32, (cg, cg), 1)
    causal_g = rowg >= colg
    heads = range(N_HEADS)
    subs = range(c // cg)
    rows = lambda j: slice(j * cg, (j + 1) * cg)
    cols = lambda h: slice(h * DK, (h + 1) * DK)
    bcum = _tri_cumsum(trig, log_alpha)
    b_tots = [bcum[(j + 1) * cg - 1:(j + 1) * cg, :] for j in subs]
    e_end = jnp.concatenate([jnp.exp(b_tots[j] - bcum[rows(j), :]) for j in subs], axis=0)
    next_proj[0]()
    q_all = p_ref[:, 0:N_HEADS * DK].astype(F32) * qscale
    k_all = p_ref[:, N_HEADS * DK:2 * N_HEADS * DK].astype(F32)
    qe = (q_all * jnp.exp(bcum)).astype(BF16)
    ke = (k_all * jnp.exp(-bcum)).astype(BF16)
    kd = (k_all * e_end).astype(BF16)
    vs = [[p_ref[rows(j), 1024 + h * DV:1024 + (h + 1) * DV] for h in heads] for j in subs]
    next_proj[1]()
    scs = [[jnp.where(causal_g, _dot_nt(qe[rows(j), cols(h)], ke[rows(j), cols(h)]), 0.0).astype(BF16)
            for h in heads] for j in subs]
    next_proj[2]()
    y_in = [[_dot(scs[j][h], vs[j][h]) for h in heads] for j in subs]
    next_proj[3]()
    sts = [gs_ref[h] for h in heads]
    ys = [[] for _ in heads]
    for j in subs:
        for h in heads:
            ys[h].append(y_in[j][h] + _dot_nt(qe[rows(j), cols(h)], sts[h].astype(BF16)))
        e_tot = jnp.exp(b_tots[j])
        sts = [e_tot[:, cols(h)] * sts[h] + _dot_tn(vs[j][h], kd[rows(j), cols(h)]) for h in heads]
    for h in heads:
        gs_ref[h] = sts[h]
    yns = [_head_norm(jnp.concatenate(ys[h], axis=0), gng_ref[:, h * DV:(h + 1) * DV], False)
           * _silu(p_ref[:, 2048 + h * DV:2048 + (h + 1) * DV].astype(F32)) for h in heads]
    for h in heads:
        z_ref[...] += _dot(yns[h].astype(BF16), wo_ref[h * DV:(h + 1) * DV, :])

    u = p_ref[:, 4096:6144].astype(F32)
    xbc = _silu(_causal_conv_tile(hist_ref, u, cw_ref, cb_ref, c))
    lane = lax.broadcasted_iota(jnp.int32, small.shape, 1)
    is_dt = (lane // GLA_RANK) == 1
    dt = jnp.where(is_dt, _softplus(small + dtb_ref[...]), 0.0)
    la = dt * -jnp.exp(alog_ref[...])
    cum = _tri_cumsum(tri_ref[...], la)
    pm = jnp.where(lane < GLA_RANK, _dt_shift(dt), cum)
    pmt = pm.T
    row = lax.broadcasted_iota(jnp.int32, (c, c), 0)
    col = lax.broadcasted_iota(jnp.int32, (c, c), 1)
    causal = row >= col
    for g in range(SSD_G):
        if N_HEADS + g < len(next_proj):
            next_proj[N_HEADS + g]()
        bm = xbc[:, 1024 + g * SSD_N:1024 + (g + 1) * SSD_N].astype(BF16)
        cm = xbc[:, 1536 + g * SSD_N:1536 + (g + 1) * SSD_N].astype(BF16)
        cb = _dot_nt(cm, bm)
        stg = ss_ref[g]
        y_inter = _dot(cm, stg.astype(BF16))
        hds = [g * SSD_HPG + r for r in range(SSD_HPG)]
        x_hs = [xbc[:, hd * SSD_P:(hd + 1) * SSD_P] for hd in hds]
        dt_cols = [pm[:, hd:hd + 1] for hd in hds]
        cum_cols = [pm[:, GLA_RANK + hd:GLA_RANK + hd + 1] for hd in hds]
        ws = [cb * jnp.exp(jnp.where(causal, cum_cols[r] - pmt[GLA_RANK + hd:GLA_RANK + hd + 1, :], NEG_BIG))
              for r, hd in enumerate(hds)]
        y_hs = [_dot(ws[r].astype(BF16), (x_hs[r] * dt_cols[r]).astype(BF16)) for r in range(SSD_HPG)]
        y_parts = [y_hs[r] + jnp.exp(cum_cols[r]) * y_inter[:, r * SSD_P:(r + 1) * SSD_P]
                   + dskip_ref[:, hd * SSD_P:(hd + 1) * SSD_P] * x_hs[r] for r, hd in enumerate(hds)]
        lasts = [cum_cols[r][c - 1:c, :] for r in range(SSD_HPG)]
        xs_parts = [x_hs[r] * (dt_cols[r] * jnp.exp(lasts[r] - cum_cols[r])) for r in range(SSD_HPG)]
        dec_parts = [jnp.broadcast_to(jnp.exp(lasts[r]), (1, SSD_P)) for r in range(SSD_HPG)]
        xs = jnp.concatenate(xs_parts, axis=1).astype(BF16)
        dec = jnp.concatenate(dec_parts, axis=1)
        ss_ref[g] = dec * stg + _dot_tn(bm, xs)
        yg = jnp.concatenate(y_parts, axis=1) * _silu(p_ref[:, 3072 + g * DV:3072 + (g + 1) * DV].astype(F32))
        yn = _head_norm(yg, sng_ref[:, g * DV:(g + 1) * DV], False)
        z_ref[...] += _dot(yn.astype(BF16), wo_ref[D_MODEL + g * DV:D_MODEL + (g + 1) * DV, :])

    o_ref[...] = _layernorm(z_ref[...], lg_ref[...], lb_ref[...])


def _dt_shift(dt):
    return pltpu.roll(dt, LANES - GLA_RANK, 1)


def _layer_cd(x, w_main, w_small, tri, trig, wal, bal, gla_g, conv_w, conv_b, dtb, alog, dskip, ssd_g,
              w_out, ln_g, ln_b, b, s, c, cg):
    nt = s // c
    t, d = x.shape
    n = b * nt
    tok = lambda i: (i, 0)
    nxt = lambda i: (jnp.minimum(i + 1, n - 1), 0)
    full2 = lambda i: (0, 0)
    ch = D_MODEL + 2 * SSD_G * SSD_N
    return pl.pallas_call(
        functools.partial(_layer_cd_kernel, c=c, cg=cg, nt=nt),
        grid=(n,),
        in_specs=[
            pl.BlockSpec((c, d), tok),
            pl.BlockSpec((c, d), nxt),
            pl.BlockSpec((d, P_MAIN), full2),
            pl.BlockSpec((d, LANES), full2),
            pl.BlockSpec((c, c), full2),
            pl.BlockSpec((c, c), full2),
            pl.BlockSpec((LANES, N_HEADS * DK), full2),
            pl.BlockSpec((1, N_HEADS * DK), full2),
            pl.BlockSpec((1, D_MODEL), full2),
            pl.BlockSpec((CONV_K, ch), full2),
            pl.BlockSpec((1, ch), full2),
            pl.BlockSpec((1, LANES), full2),
            pl.BlockSpec((1, LANES), full2),
            pl.BlockSpec((1, D_MODEL), full2),
            pl.BlockSpec((1, D_MODEL), full2),
            pl.BlockSpec((2 * D_MODEL, d), full2),
            pl.BlockSpec((1, d), full2),
            pl.BlockSpec((1, d), full2),
        ],
        out_specs=pl.BlockSpec((c, d), tok),
        out_shape=jax.ShapeDtypeStruct((t, d), F32),
        scratch_shapes=[
            pltpu.VMEM((2, c, P_MAIN), BF16),
            pltpu.VMEM((2, c, LANES), F32),
            pltpu.VMEM((c, d), BF16),
            pltpu.VMEM((c, d), F32),
            pltpu.VMEM((N_HEADS, DV, DK), F32),
            pltpu.VMEM((SSD_G, SSD_N, SSD_HPG * SSD_P), F32),
            pltpu.VMEM((SUBLANES + c, ch), F32),
        ],
        compiler_params=_cparams(("arbitrary",)),
        name="layer_cd",
    )(x, x, w_main, w_small, tri, trig, wal, bal, gla_g, conv_w, conv_b, dtb, alog, dskip, ssd_g, w_out, ln_g, ln_b)


def _split2(x):
    hi = x.astype(BF16)
    return hi, (x - hi.astype(F32)).astype(BF16)


def _router_kernel(h_ref, w_ref, b_ref, ltri_ref, utri_ref, pos_ref, gate_ref, tot_ref, carry_ref, *, cap):
    @pl.when(pl.program_id(0) == 0)
    def _():
        carry_ref[...] = jnp.zeros_like(carry_ref)

    h_hi, h_lo = _split2(h_ref[...])
    w_hi, w_lo = _split2(w_ref[...])
    logits = (_dot(h_hi, w_hi) + _dot(h_lo, w_hi) + _dot(h_hi, w_lo)).T[:N_EXPERTS, :]
    e = jnp.exp(logits - jnp.max(logits, axis=0, keepdims=True))
    probs = e / jnp.sum(e, axis=0, keepdims=True)
    sel = probs + b_ref[...]
    row = lax.broadcasted_iota(jnp.int32, sel.shape, 0)
    pos = row & (EPG - 1)
    gidx = row // EPG

    def member(a, k):
        fwd = pltpu.roll(a, N_EXPERTS - k, 0)
        back = pltpu.roll(a, EPG - k, 0)
        return jnp.where(pos + k < EPG, fwd, back)

    others = [member(sel, k) for k in range(1, EPG)]
    vals = [sel] + others
    top2 = None
    for i in range(EPG):
        for j in range(i + 1, EPG):
            pair = vals[i] + vals[j]
            top2 = pair if top2 is None else jnp.maximum(top2, pair)
    best = jnp.ones(sel.shape, jnp.int32)
    for k in range(1, N_GROUPS):
        other = pltpu.roll(top2, EPG * k, 0)
        wins = jnp.where(gidx >= k, jnp.where(top2 > other, 1, 0), jnp.where(top2 >= other, 1, 0))
        best = best * wins
    rank = jnp.zeros(sel.shape, jnp.int32)
    for k in range(1, EPG):
        v = others[k - 1]
        tie = jnp.where(pos + k >= EPG, 1, 0)
        rank = rank + jnp.where(v > sel, 1, jnp.where(v == sel, tie, 0))
    chosen = jnp.where(best * jnp.where(rank < 2, 1, 0) > 0, 1.0, 0.0)
    gsel = chosen * probs
    comb = gsel / jnp.sum(gsel, axis=0, keepdims=True)

    chosen_b = chosen.astype(BF16)
    lower = _dot(ltri_ref[...], chosen_b)
    first = chosen * jnp.where(lower == 0.0, 1.0, 0.0)
    second = chosen - first
    carry = carry_ref[:, 0:1]
    before = _dot(chosen_b, utri_ref[...])
    slot = carry + before + row.astype(F32) * float(cap)
    pos0 = jnp.sum(first * slot, axis=0, keepdims=True)
    pos1 = jnp.sum(second * slot, axis=0, keepdims=True)
    g0 = jnp.sum(first * comb, axis=0, keepdims=True)
    g1 = jnp.sum(second * comb, axis=0, keepdims=True)
    r8 = lax.broadcasted_iota(jnp.int32, (SUBLANES, sel.shape[1]), 0)
    pos_ref[...] = jnp.where(r8 == 0, pos0, jnp.where(r8 == 1, pos1, 0.0)).astype(jnp.int32)
    gate_ref[...] = jnp.where(r8 == 0, g0, jnp.where(r8 == 1, g1, 0.0))
    total = carry_ref[...] + jnp.sum(chosen, axis=1, keepdims=True)
    carry_ref[...] = total
    tot_ref[...] = total.astype(jnp.int32)


def _router(h, w_pad, b_col, cap, tm=1024):
    t, d = h.shape
    tm = min(tm, t)
    ltri = jnp.asarray(np.tril(np.ones((N_EXPERTS, N_EXPERTS), np.float32), -1), BF16)
    utri = jnp.asarray(np.triu(np.ones((tm, tm), np.float32), 1), BF16)
    return pl.pallas_call(
        functools.partial(_router_kernel, cap=cap),
        grid=(t // tm,),
        in_specs=[
            pl.BlockSpec((tm, d), lambda i: (i, 0)),
            pl.BlockSpec((d, LANES), lambda i: (0, 0)),
            pl.BlockSpec((N_EXPERTS, 1), lambda i: (0, 0)),
            pl.BlockSpec((N_EXPERTS, N_EXPERTS), lambda i: (0, 0)),
            pl.BlockSpec((tm, tm), lambda i: (0, 0)),
        ],
        out_specs=[
            pl.BlockSpec((SUBLANES, tm), lambda i: (0, i)),
            pl.BlockSpec((SUBLANES, tm), lambda i: (0, i)),
            pl.BlockSpec((N_EXPERTS, LANES), lambda i: (0, 0)),
        ],
        out_shape=[
            jax.ShapeDtypeStruct((SUBLANES, t), jnp.int32),
            jax.ShapeDtypeStruct((SUBLANES, t), F32),
            jax.ShapeDtypeStruct((N_EXPERTS, LANES), jnp.int32),
        ],
        scratch_shapes=[pltpu.VMEM((N_EXPERTS, LANES), F32)],
        compiler_params=_cparams(("arbitrary",)),
        name="router",
    )(h, w_pad, b_col, ltri, utri)


def _rebase_kernel(base_ref, pos_ref, o_ref, *, cap):
    p = pos_ref[...]
    e = p // cap
    out = p - e * cap
    for k in range(N_EXPERTS):
        out = out + jnp.where(e == k, base_ref[k], 0)
    o_ref[...] = out


def _rebase(pos, base, cap, tm=8192):
    t = pos.shape[1]
    tm = min(tm, t)
    return pl.pallas_call(
        functools.partial(_rebase_kernel, cap=cap),
        grid_spec=pltpu.PrefetchScalarGridSpec(
            num_scalar_prefetch=1,
            grid=(t // tm,),
            in_specs=[pl.BlockSpec((SUBLANES, tm), lambda i, base: (0, i))],
            out_specs=pl.BlockSpec((SUBLANES, tm), lambda i, base: (0, i)),
        ),
        out_shape=jax.ShapeDtypeStruct(pos.shape, jnp.int32),
        compiler_params=_cparams(("arbitrary",)),
        name="rebase",
    )(base, pos)


def _dispatch_kernel(lo_ref, hi_ref, pos_ref, h_ref, xs_ref, sem, *, tm):
    def issue(r, carry):
        src = h_ref.at[pl.ds(r, 1), :]
        pltpu.make_async_copy(src, xs_ref.at[pl.ds(pos_ref[0, r], 1), :], sem).start()
        pltpu.make_async_copy(src, xs_ref.at[pl.ds(pos_ref[1, r], 1), :], sem).start()
        return carry

    lax.fori_loop(0, tm, issue, 0, unroll=8)

    @pl.when(pl.program_id(0) == pl.num_programs(0) - 1)
    def _():
        src = h_ref.at[pl.ds(0, 1), :]

        def fill(r, carry):
            pltpu.make_async_copy(src, xs_ref.at[pl.ds(r, 1), :], sem).start()
            return carry

        def drain(r, carry):
            pltpu.make_async_copy(src, xs_ref.at[pl.ds(0, 1), :], sem).wait()
            return carry

        for k in range(N_EXPERTS + 1):
            lax.fori_loop(lo_ref[k], hi_ref[k], fill, 0)
            lax.fori_loop(lo_ref[k], hi_ref[k], drain, 0)

    whole = pltpu.make_async_copy(h_ref, xs_ref.at[pl.ds(0, tm), :], sem)
    whole.wait()
    whole.wait()


def _dispatch(h, pos, fill_lo, fill_hi, n_rows, tm=1024):
    t, d = h.shape
    tm = min(tm, t)
    return pl.pallas_call(
        functools.partial(_dispatch_kernel, tm=tm),
        grid_spec=pltpu.PrefetchScalarGridSpec(
            num_scalar_prefetch=2,
            grid=(t // tm,),
            in_specs=[
                pl.BlockSpec((SUBLANES, tm), lambda i, lo, hi: (0, i), memory_space=pltpu.SMEM),
                pl.BlockSpec((tm, d), lambda i, lo, hi: (i, 0)),
            ],
            out_specs=pl.BlockSpec(memory_space=pl.ANY),
            scratch_shapes=[pltpu.SemaphoreType.DMA],
        ),
        out_shape=jax.ShapeDtypeStruct((n_rows, d), F32),
        compiler_params=_cparams(("arbitrary",)),
        name="dispatch",
    )(fill_lo, fill_hi, pos, h)


def _ffn_kernel(exp_ref, x_ref, wg_ref, wu_ref, wd_ref, y_ref):
    xb = x_ref[...].astype(BF16)
    act = _silu(_dot(xb, wg_ref[...])) * _dot(xb, wu_ref[...])
    y_ref[...] = _dot(act.astype(BF16), wd_ref[...])


def _ffn(xs, tile_exp, wg, wu, wd):
    n, d = xs.shape
    return pl.pallas_call(
        _ffn_kernel,
        grid_spec=pltpu.PrefetchScalarGridSpec(
            num_scalar_prefetch=1,
            grid=(n // TM_F,),
            in_specs=[
                pl.BlockSpec((TM_F, d), lambda i, ex: (i, 0)),
                pl.BlockSpec((None, d, D_FF), lambda i, ex: (ex[i], 0, 0)),
                pl.BlockSpec((None, d, D_FF), lambda i, ex: (ex[i], 0, 0)),
                pl.BlockSpec((None, D_FF, d), lambda i, ex: (ex[i], 0, 0)),
            ],
            out_specs=pl.BlockSpec((TM_F, d), lambda i, ex: (i, 0)),
        ),
        out_shape=jax.ShapeDtypeStruct((n, d), F32),
        compiler_params=_cparams(("arbitrary",)),
        name="expert_ffn",
    )(tile_exp, xs, wg, wu, wd)


def _combine_kernel(pos_ref, gate_ref, h_ref, ys_ref, g_ref, b_ref, o_ref, buf_ref, sem, *, tm):
    def issue(r, carry):
        pltpu.make_async_copy(ys_ref.at[pl.ds(pos_ref[0, r], 1), :], buf_ref.at[0, pl.ds(r, 1), :], sem).start()
        pltpu.make_async_copy(ys_ref.at[pl.ds(pos_ref[1, r], 1), :], buf_ref.at[1, pl.ds(r, 1), :], sem).start()
        return carry

    lax.fori_loop(0, tm, issue, 0, unroll=8)
    gt = gate_ref[...].T
    for k in range(2):
        pltpu.make_async_copy(ys_ref.at[pl.ds(0, tm), :], buf_ref.at[k], sem).wait()
    z = ALPHA * h_ref[...] + gt[:, 0:1] * buf_ref[0] + gt[:, 1:2] * buf_ref[1]
    o_ref[...] = _layernorm(z, g_ref[...], b_ref[...])


def _combine(h, pos, gates, ys, g, b, tm=1024):
    t, d = h.shape
    tm = min(tm, t)
    return pl.pallas_call(
        functools.partial(_combine_kernel, tm=tm),
        grid=(t // tm,),
        in_specs=[
            pl.BlockSpec((SUBLANES, tm), lambda i: (0, i), memory_space=pltpu.SMEM),
            pl.BlockSpec((SUBLANES, tm), lambda i: (0, i)),
            pl.BlockSpec((tm, d), lambda i: (i, 0)),
            pl.BlockSpec(memory_space=pl.ANY),
            pl.BlockSpec((1, d), lambda i: (0, 0)),
            pl.BlockSpec((1, d), lambda i: (0, 0)),
        ],
        out_specs=pl.BlockSpec((tm, d), lambda i: (i, 0)),
        out_shape=jax.ShapeDtypeStruct((t, d), F32),
        scratch_shapes=[pltpu.VMEM((2, tm, d), F32), pltpu.SemaphoreType.DMA],
        compiler_params=_cparams(("arbitrary",)),
        name="combine_ln",
    )(pos, gates, h, ys, g, b)


def _moe_ln(h, w_pad, b_col, wg, wu, wd, g, b):
    t, d = h.shape
    cap = 2 * t
    nt = (2 * t) // TM_F + N_EXPERTS
    pos, gates, tot = _router(h, w_pad, b_col, cap)
    totals = tot[:, 0]
    ntile = (totals + TM_F - 1) // TM_F
    ends = jnp.cumsum(ntile)
    base = (ends - ntile) * TM_F
    tile_exp = jnp.minimum(jnp.sum((jnp.arange(nt, dtype=jnp.int32)[:, None] >= ends[None, :]).astype(jnp.int32),
                                   axis=1), N_EXPERTS - 1)
    fill_lo = jnp.concatenate([base + totals, ends[-1:] * TM_F]).astype(jnp.int32)
    fill_hi = jnp.concatenate([ends * TM_F, jnp.full((1,), nt * TM_F, jnp.int32)]).astype(jnp.int32)
    pos = _rebase(pos, base.astype(jnp.int32), cap)
    xs = _dispatch(h, pos, fill_lo, fill_hi, nt * TM_F)
    ys = _ffn(xs, tile_exp.astype(jnp.int32), wg, wu, wd)
    return _combine(h, pos, gates, ys, g, b)


def _retention_tables(c):
    log_g = np.log(1.0 - 2.0 ** (-5.0 - np.arange(N_HEADS, dtype=np.float32))).astype(np.float32)
    idx = np.arange(c, dtype=np.float32)
    diff = idx[:, None] - idx[None, :]
    dmask = np.where(diff >= 0, np.exp(log_g[:, None, None] * np.maximum(diff, 0.0)), 0.0)
    qdec = np.exp(log_g[:, None] * (idx + 1.0))[..., None] * np.ones((1, 1, DK), np.float32)
    kdec = np.exp(log_g[:, None] * (c - 1.0 - idx))[..., None] * np.ones((1, 1, DK), np.float32)
    cdec = np.exp(log_g * c)[:, None, None] * np.ones((1, 1, DV), np.float32)
    return tuple(jnp.asarray(a, F32) for a in (dmask, qdec, kdec, cdec))


def _tri(c, block=None):
    block = c if block is None else block
    return jnp.asarray(np.kron(np.eye(c // block, dtype=np.float32), np.tril(np.ones((block, block), np.float32))),
                       BF16)


def _pad_cols(w, n):
    return jnp.pad(w, ((0, 0), (0, n - w.shape[1])))


def kernel(x, positions, w_in_ab, ret_norm_g, ml_conv_w, ml_conv_b, ml_b_i, ml_b_f, ml_norm_g, w_out_ab,
           w_in_cd, gla_w_alpha, gla_b_alpha, gla_norm_g, ssd_conv_w, ssd_conv_b, ssd_dt_bias, ssd_a_log, ssd_d,
           ssd_norm_g, w_out_cd, w_router, b_router, moe_w_gate, moe_w_up, moe_w_down,
           ln_mix_g, ln_mix_b, ln_ffn_g, ln_ffn_b):
    b, s, d = x.shape
    t = b * s
    c = min(256, s)
    cg = min(64, c)
    h = x.reshape(t, d)
    row = lambda v: v.reshape(1, -1).astype(F32)

    w_pad = _pad_cols(w_router.astype(F32), LANES)
    b_col = b_router.astype(F32).reshape(N_EXPERTS, 1)
    wg = moe_w_gate.astype(BF16)
    wu = moe_w_up.astype(BF16)
    wd = moe_w_down.astype(BF16)

    def ffn(hh, layer):
        return _moe_ln(hh, w_pad, b_col, wg[layer], wu[layer], wd[layer], row(ln_ffn_g[layer]), row(ln_ffn_b[layer]))

    half = np.concatenate([np.arange(0, DK, 2), np.arange(1, DK, 2)])
    perm = np.concatenate([hd * DK + half for hd in range(N_HEADS)])
    w = w_in_ab[0]
    nqk = N_HEADS * DK
    w_main = jnp.concatenate([w[:, :nqk][:, perm], w[:, nqk:2 * nqk][:, perm], w[:, 2 * nqk:P_MAIN]], axis=1)
    inv = ROPE_BASE ** (-jnp.arange(0, DK, 2, dtype=F32) / DK)
    cc, ss = _rope_tables(positions.reshape(t, 1), jnp.concatenate([inv, inv]).reshape(1, DK))
    gate_b = _pad_cols(jnp.concatenate([ml_b_i[0], ml_b_f[0]]).reshape(1, -1).astype(F32), LANES)
    h = _layer_ab(h, w_main.astype(BF16), _pad_cols(w[:, P_MAIN:], LANES).astype(BF16), cc, ss,
                  _retention_tables(c) + (_tri(c),), row(ret_norm_g[0]), row(ml_norm_g[0]),
                  ml_conv_w[0].astype(F32), row(ml_conv_b[0]), gate_b,
                  w_out_ab[0].astype(BF16), row(ln_mix_g[0]), row(ln_mix_b[0]), b, s, c)
    h = ffn(h, 0)

    w = w_in_cd[0]
    g0 = 2 * nqk + 2 * N_HEADS * DV
    s0 = g0 + GLA_RANK
    s1 = s0 + D_MODEL + D_MODEL + 2 * SSD_G * SSD_N
    w_main = jnp.concatenate([w[:, :g0], w[:, s0:s1]], axis=1)
    w_small = _pad_cols(jnp.concatenate([w[:, g0:s0], w[:, s1:]], axis=1), LANES)
    wal = jnp.pad(gla_w_alpha[0], ((0, LANES - GLA_RANK), (0, 0))).astype(BF16)
    lane_pad = lambda v: jnp.pad(v.reshape(1, -1).astype(F32), ((0, 0), (GLA_RANK, LANES - GLA_RANK - SSD_H)))
    dskip = jnp.repeat(ssd_d[0].astype(F32), SSD_P).reshape(1, -1)
    h = _layer_cd(h, w_main.astype(BF16), w_small.astype(BF16), _tri(c), _tri(c, cg), wal, row(gla_b_alpha[0]),
                  row(gla_norm_g[0]), ssd_conv_w[0].astype(F32), row(ssd_conv_b[0]), lane_pad(ssd_dt_bias[0]),
                  lane_pad(ssd_a_log[0]), dskip, row(ssd_norm_g[0]),
                  w_out_cd[0].astype(BF16), row(ln_mix_g[1]), row(ln_mix_b[1]), b, s, c, cg)
    h = ffn(h, 1)
    return h.reshape(b, s, d)
```

```python
import functools
import math

import jax
import jax.numpy as jnp
import numpy as np
from jax import lax
from jax.experimental import pallas as pl
from jax.experimental.pallas import tpu as pltpu

F32 = jnp.float32
BF16 = jnp.bfloat16

D_MODEL = 1024
DEPTH = 2
LN_EPS = 1e-5
NORM_EPS = 1e-6
N_HEADS = 4
DK = D_MODEL // 8
DV = D_MODEL // 4
ROPE_BASE = 10000.0
CONV_K = 4
GLA_RANK = 16
GLA_TAU = 16.0
SSD_P = 64
SSD_H = D_MODEL // SSD_P
SSD_G = 4
SSD_HPG = SSD_H // SSD_G
SSD_N = 128
N_EXPERTS = 16
N_GROUPS = 4
EPG = N_EXPERTS // N_GROUPS
D_FF = D_MODEL // 2
ALPHA = (2.0 * DEPTH) ** 0.25

LANES = 128
SUBLANES = 8
VMEM_LIMIT = 56 * 1024 * 1024

P_MAIN = 6 * D_MODEL
TM_F = 512
SSD_GB = 1
PROJ_PIECE = 256
PROJ_SITES = 7
NEG_BIG = -1e30


def _cparams(sem):
    return pltpu.CompilerParams(dimension_semantics=sem, vmem_limit_bytes=VMEM_LIMIT)


def _dot(a, b):
    return jnp.dot(a, b, preferred_element_type=F32)


def _dot_nt(a, b):
    return lax.dot_general(a, b, (((1,), (1,)), ((), ())), preferred_element_type=F32)


def _dot_tn(a, b):
    return lax.dot_general(a, b, (((0,), (0,)), ((), ())), preferred_element_type=F32)


def _sigmoid(x):
    return 1.0 / (1.0 + jnp.exp(-x))


def _silu(x):
    return x * _sigmoid(x)


def _log_sigmoid(x):
    return jnp.minimum(x, 0.0) - jnp.log(1.0 + jnp.exp(-jnp.abs(x)))


def _softplus(x):
    return jnp.maximum(x, 0.0) + jnp.log(1.0 + jnp.exp(-jnp.abs(x)))


def _tri_cumsum(tri, x):
    hi = x.astype(BF16)
    r1 = x - hi.astype(F32)
    mid = r1.astype(BF16)
    lo = (r1 - mid.astype(F32)).astype(BF16)
    return _dot(tri, hi) + _dot(tri, mid) + _dot(tri, lo)


def _head_norm(y, g, center):
    if center:
        y = y - jnp.mean(y, axis=-1, keepdims=True)
    return y * lax.rsqrt(jnp.mean(y * y, axis=-1, keepdims=True) + NORM_EPS) * g


def _in_projection(x_ref, wm_ref, ws_ref, p_ref, s_ref):
    xb = x_ref[...].astype(BF16)
    for j in range(P_MAIN // D_MODEL):
        sl = slice(j * D_MODEL, (j + 1) * D_MODEL)
        p_ref[:, sl] = _dot(xb, wm_ref[:, sl]).astype(BF16)
    s_ref[...] = _dot(xb, ws_ref[...])


def _pipelined_in_projection(x_ref, xn_ref, wm_ref, ws_ref, pbuf_ref, sbuf_ref, xb_ref):
    step = pl.program_id(0)
    slot = step % 2

    @pl.when(step == 0)
    def _():
        _in_projection(x_ref, wm_ref, ws_ref, pbuf_ref.at[0], sbuf_ref.at[0])

    nxt_p = pbuf_ref.at[1 - slot]
    nxt_s = sbuf_ref.at[1 - slot]
    xb_ref[...] = xn_ref[...].astype(BF16)

    def main_piece(j):
        def run():
            sl = slice(j * PROJ_PIECE, (j + 1) * PROJ_PIECE)
            nxt_p[:, sl] = _dot(xb_ref[...], wm_ref[:, sl]).astype(BF16)
        return run

    def small_piece():
        nxt_s[...] = _dot(xb_ref[...], ws_ref[...])

    thunks = [main_piece(j) for j in range(P_MAIN // PROJ_PIECE)] + [small_piece]

    def advance(site):
        for k in range(site * len(thunks) // PROJ_SITES, (site + 1) * len(thunks) // PROJ_SITES):
            thunks[k]()

    return pbuf_ref.at[slot], sbuf_ref[slot], advance


def _layernorm(z, g, b):
    zc = z - jnp.mean(z, axis=-1, keepdims=True)
    return zc * lax.rsqrt(jnp.mean(zc * zc, axis=-1, keepdims=True) + LN_EPS) * g + b


def _rope_kernel(pos_ref, inv_ref, cc_ref, ss_ref):
    ang = pos_ref[...].astype(F32) * inv_ref[...]
    lane = lax.broadcasted_iota(jnp.int32, ang.shape, 1)
    s = jnp.sin(ang)
    cc_ref[...] = jnp.cos(ang)
    ss_ref[...] = jnp.where(lane < DK // 2, -s, s)


def _rope_tables(pos_col, inv2, tm=2048):
    t = pos_col.shape[0]
    tm = min(tm, t)
    return pl.pallas_call(
        _rope_kernel,
        grid=(t // tm,),
        in_specs=[pl.BlockSpec((tm, 1), lambda i: (i, 0)), pl.BlockSpec((1, DK), lambda i: (0, 0))],
        out_specs=[pl.BlockSpec((tm, DK), lambda i: (i, 0))] * 2,
        out_shape=[jax.ShapeDtypeStruct((t, DK), F32)] * 2,
        compiler_params=_cparams(("parallel",)),
        name="rope_tables",
    )(pos_col, inv2)


def _causal_conv_tile(hist_ref, u, w_ref, b_ref, c):
    hist_ref[pl.ds(SUBLANES, c), :] = u
    acc = b_ref[...] + w_ref[CONV_K - 1:CONV_K, :] * u
    for j in range(CONV_K - 1):
        acc = acc + w_ref[j:j + 1, :] * hist_ref[pl.ds(SUBLANES - (CONV_K - 1) + j, c), :]
    hist_ref[pl.ds(0, SUBLANES), :] = hist_ref[pl.ds(c, SUBLANES), :]
    return acc


def _layer_ab_kernel(x_ref, xn_ref, wm_ref, ws_ref, cc_ref, ss_ref, dmask_ref, qdec_ref, kdec_ref, cdec_ref,
                     tri_ref, rng_ref, mng_ref, cw_ref, cb_ref, gb_ref, wo_ref, lg_ref, lb_ref, o_ref,
                     pbuf_ref, sbuf_ref, xb_ref, z_ref, rs_ref, mc_ref, mn_ref, mm_ref, hist_ref, *, c, nt):
    @pl.when(pl.program_id(0) % nt == 0)
    def _():
        rs_ref[...] = jnp.zeros_like(rs_ref)
        mc_ref[...] = jnp.zeros_like(mc_ref)
        mn_ref[...] = jnp.zeros_like(mn_ref)
        mm_ref[...] = jnp.zeros_like(mm_ref)
        hist_ref[pl.ds(0, SUBLANES), :] = jnp.zeros((SUBLANES, hist_ref.shape[1]), F32)

    p_ref, small, next_proj = _pipelined_in_projection(x_ref, xn_ref, wm_ref, ws_ref, pbuf_ref, sbuf_ref, xb_ref)
    z_ref[...] = ALPHA * x_ref[...]
    cc = cc_ref[...]
    ss = ss_ref[...]
    kscale = DK ** -0.5

    def rot(t):
        return t * cc + pltpu.roll(t, DK // 2, 1) * ss

    heads = range(N_HEADS)
    qs = [rot(p_ref[:, h * DK:(h + 1) * DK].astype(F32)) for h in heads]
    ks = [rot(p_ref[:, 512 + h * DK:512 + (h + 1) * DK].astype(F32)) * kscale for h in heads]
    vs = [p_ref[:, 1024 + h * DV:1024 + (h + 1) * DV] for h in heads]
    next_proj(0)
    scs = [_dot_nt(qs[h].astype(BF16), ks[h].astype(BF16)) * dmask_ref[h] for h in heads]
    next_proj(1)
    ys = [_dot(scs[h].astype(BF16), vs[h]) + _dot((qs[h] * qdec_ref[h]).astype(BF16), rs_ref[h].astype(BF16))
          for h in heads]
    next_proj(2)
    for h in heads:
        rs_ref[h] = cdec_ref[h] * rs_ref[h] + _dot_tn((ks[h] * kdec_ref[h]).astype(BF16), vs[h])
    yns = [_head_norm(ys[h], rng_ref[:, h * DV:(h + 1) * DV], True)
           * _silu(p_ref[:, 2048 + h * DV:2048 + (h + 1) * DV].astype(F32)) for h in heads]
    next_proj(3)
    for h in heads:
        z_ref[...] += _dot(yns[h].astype(BF16), wo_ref[h * DV:(h + 1) * DV, :])

    u = p_ref[:, 3072:4096].astype(F32)
    qk = _silu(_causal_conv_tile(hist_ref, u, cw_ref, cb_ref, c))
    gates = small + gb_ref[...]
    lane = lax.broadcasted_iota(jnp.int32, gates.shape, 1)
    is_f = (lane // N_HEADS) == 1
    logf = jnp.where(is_f, _log_sigmoid(gates), 0.0)
    bcum = _tri_cumsum(tri_ref[...], logf)
    pm = jnp.where(lane < N_HEADS, gates, bcum)
    pmt = pm.T
    row = lax.broadcasted_iota(jnp.int32, (c, c), 0)
    col = lax.broadcasted_iota(jnp.int32, (c, c), 1)
    causal = row >= col
    qs = [qk[:, h * DK:(h + 1) * DK] for h in heads]
    ks = [qk[:, 512 + h * DK:512 + (h + 1) * DK] * kscale for h in heads]
    vs = [p_ref[:, 4096 + h * DV:4096 + (h + 1) * DV] for h in heads]
    i_cols = [pm[:, h:h + 1] for h in heads]
    b_cols = [pm[:, N_HEADS + h:N_HEADS + h + 1] for h in heads]
    m_prevs = [mm_ref[h][:, 0:1] for h in heads]
    a_s = [b_cols[h] + m_prevs[h] for h in heads]
    ds = [jnp.where(causal, b_cols[h] - pmt[N_HEADS + h:N_HEADS + h + 1, :] + pmt[h:h + 1, :], NEG_BIG)
          for h in heads]
    m_ts = [jnp.maximum(a_s[h], jnp.max(ds[h], axis=1, keepdims=True)) for h in heads]
    w_inters = [jnp.exp(a_s[h] - m_ts[h]) for h in heads]
    next_proj(4)
    s_qks = [_dot_nt(qs[h].astype(BF16), ks[h].astype(BF16)) * jnp.exp(ds[h] - m_ts[h]) for h in heads]
    next_proj(5)
    nums = [_dot(s_qks[h].astype(BF16), vs[h]) + w_inters[h] * _dot(qs[h].astype(BF16), mc_ref[h].astype(BF16))
            for h in heads]
    dens = [jnp.sum(s_qks[h], axis=1, keepdims=True)
            + w_inters[h] * jnp.sum(qs[h] * mn_ref[h], axis=1, keepdims=True) for h in heads]
    hhs = [nums[h] / jnp.maximum(jnp.abs(dens[h]), jnp.exp(-m_ts[h])) for h in heads]
    next_proj(6)
    for h in heads:
        b_tot = b_cols[h][c - 1:c, :]
        lw = b_tot - b_cols[h] + i_cols[h]
        m_new = jnp.maximum(b_tot + m_prevs[h], jnp.max(lw, axis=0, keepdims=True))
        kd = ks[h] * jnp.exp(lw - m_new)
        decay = jnp.exp(b_tot + m_prevs[h] - m_new)
        mc_ref[h] = decay * mc_ref[h] + _dot_tn(kd.astype(BF16), vs[h])
        mn_ref[h] = decay * mn_ref[h] + jnp.sum(kd, axis=0, keepdims=True)
        mm_ref[h] = jnp.broadcast_to(m_new, (1, LANES))
    yns = [_head_norm(hhs[h] * _sigmoid(p_ref[:, 5120 + h * DV:5120 + (h + 1) * DV].astype(F32)),
                      mng_ref[:, h * DV:(h + 1) * DV], True) for h in heads]
    for h in heads:
        z_ref[...] += _dot(yns[h].astype(BF16), wo_ref[D_MODEL + h * DV:D_MODEL + (h + 1) * DV, :])

    o_ref[...] = _layernorm(z_ref[...], lg_ref[...], lb_ref[...])


def _layer_ab(x, w_main, w_small, cc, ss, tabs, ret_g, ml_g, conv_w, conv_b, gate_b, w_out, ln_g, ln_b, b, s, c):
    nt = s // c
    t, d = x.shape
    n = b * nt
    tok = lambda i: (i, 0)
    nxt = lambda i: (jnp.minimum(i + 1, n - 1), 0)
    full2 = lambda i: (0, 0)
    full3 = lambda i: (0, 0, 0)
    dmask, qdec, kdec, cdec, tri = tabs
    return pl.pallas_call(
        functools.partial(_layer_ab_kernel, c=c, nt=nt),
        grid=(n,),
        in_specs=[
            pl.BlockSpec((c, d), tok),
            pl.BlockSpec((c, d), nxt),
            pl.BlockSpec((d, P_MAIN), full2),
            pl.BlockSpec((d, LANES), full2),
            pl.BlockSpec((c, DK), tok),
            pl.BlockSpec((c, DK), tok),
            pl.BlockSpec((N_HEADS, c, c), full3),
            pl.BlockSpec((N_HEADS, c, DK), full3),
            pl.BlockSpec((N_HEADS, c, DK), full3),
            pl.BlockSpec((N_HEADS, 1, DV), full3),
            pl.BlockSpec((c, c), full2),
            pl.BlockSpec((1, D_MODEL), full2),
            pl.BlockSpec((1, D_MODEL), full2),
            pl.BlockSpec((CONV_K, D_MODEL), full2),
            pl.BlockSpec((1, D_MODEL), full2),
            pl.BlockSpec((1, LANES), full2),
            pl.BlockSpec((2 * D_MODEL, d), full2),
            pl.BlockSpec((1, d), full2),
            pl.BlockSpec((1, d), full2),
        ],
        out_specs=pl.BlockSpec((c, d), tok),
        out_shape=jax.ShapeDtypeStruct((t, d), F32),
        scratch_shapes=[
            pltpu.VMEM((2, c, P_MAIN), BF16),
            pltpu.VMEM((2, c, LANES), F32),
            pltpu.VMEM((c, d), BF16),
            pltpu.VMEM((c, d), F32),
            pltpu.VMEM((N_HEADS, DK, DV), F32),
            pltpu.VMEM((N_HEADS, DK, DV), F32),
            pltpu.VMEM((N_HEADS, 1, DK), F32),
            pltpu.VMEM((N_HEADS, 1, LANES), F32),
            pltpu.VMEM((SUBLANES + c, D_MODEL), F32),
        ],
        compiler_params=_cparams(("arbitrary",)),
        name="layer_ab",
    )(x, x, w_main, w_small, cc, ss, dmask, qdec, kdec, cdec, tri, ret_g, ml_g, conv_w, conv_b, gate_b,
      w_out, ln_g, ln_b)


def _layer_cd_kernel(x_ref, xn_ref, wm_ref, ws_ref, tri_ref, trig_ref, wal_ref, bal_ref, gng_ref, cw_ref, cb_ref,
                     dtb_ref, alog_ref, dskip_ref, sng_ref, wo_ref, lg_ref, lb_ref, o_ref,
                     pbuf_ref, sbuf_ref, xb_ref, z_ref, gs_ref, ss_ref, hist_ref, *, c, cg, nt):
    @pl.when(pl.program_id(0) % nt == 0)
    def _():
        gs_ref[...] = jnp.zeros_like(gs_ref)
        ss_ref[...] = jnp.zeros_like(ss_ref)
        hist_ref[pl.ds(0, SUBLANES), :] = jnp.zeros((SUBLANES, hist_ref.shape[1]), F32)

    p_ref, small, next_proj = _pipelined_in_projection(x_ref, xn_ref, wm_ref, ws_ref, pbuf_ref, sbuf_ref, xb_ref)
    z_ref[...] = ALPHA * x_ref[...]

    log_alpha = _log_sigmoid(_dot(small.astype(BF16), wal_ref[...]) + bal_ref[...]) * (1.0 / GLA_TAU)
    qscale = DK ** -0.5
    trig = trig_ref[...]
    rowg = lax.broadcasted_iota(jnp.int32, (cg, cg), 0)
    colg = lax.broadcasted_iota(jnp.int32, (cg, cg), 1)
    causal_g = rowg >= colg
    heads = range(N_HEADS)
    subs = range(c // cg)
    rows = lambda j: slice(j * cg, (j + 1) * cg)
    cols = lambda h: slice(h * DK, (h + 1) * DK)
    bcum = _tri_cumsum(trig, log_alpha)
    b_tots = [bcum[(j + 1) * cg - 1:(j + 1) * cg, :] for j in subs]
    e_end = jnp.concatenate([jnp.exp(b_tots[j] - bcum[rows(j), :]) for j in subs], axis=0)
    next_proj(0)
    q_all = p_ref[:, 0:N_HEADS * DK].astype(F32) * qscale
    k_all = p_ref[:, N_HEADS * DK:2 * N_HEADS * DK].astype(F32)
    qe = (q_all * jnp.exp(bcum)).astype(BF16)
    ke = (k_all * jnp.exp(-bcum)).astype(BF16)
    kd = (k_all * e_end).astype(BF16)
    vs = [[p_ref[rows(j), 1024 + h * DV:1024 + (h + 1) * DV] for h in heads] for j in subs]
    next_proj(1)
    scs = [[jnp.where(causal_g, _dot_nt(qe[rows(j), cols(h)], ke[rows(j), cols(h)]), 0.0).astype(BF16)
            for h in heads] for j in subs]
    next_proj(2)
    y_in = [[_dot(scs[j][h], vs[j][h]) for h in heads] for j in subs]
    next_proj(3)
    sts = [gs_ref[h] for h in heads]
    ys = [[] for _ in heads]
    for j in subs:
        for h in heads:
            ys[h].append(y_in[j][h] + _dot_nt(qe[rows(j), cols(h)], sts[h].astype(BF16)))
        e_tot = jnp.exp(b_tots[j])
        sts = [e_tot[:, cols(h)] * sts[h] + _dot_tn(vs[j][h], kd[rows(j), cols(h)]) for h in heads]
    for h in heads:
        gs_ref[h] = sts[h]
    yns = [_head_norm(jnp.concatenate(ys[h], axis=0), gng_ref[:, h * DV:(h + 1) * DV], False)
           * _silu(p_ref[:, 2048 + h * DV:2048 + (h + 1) * DV].astype(F32)) for h in heads]
    for h in heads:
        z_ref[...] += _dot(yns[h].astype(BF16), wo_ref[h * DV:(h + 1) * DV, :])

    u = p_ref[:, 4096:6144].astype(F32)
    xbc = _silu(_causal_conv_tile(hist_ref, u, cw_ref, cb_ref, c))
    lane = lax.broadcasted_iota(jnp.int32, small.shape, 1)
    is_dt = (lane // GLA_RANK) == 1
    dt = jnp.where(is_dt, _softplus(small + dtb_ref[...]), 0.0)
    la = dt * -jnp.exp(alog_ref[...])
    cum = _tri_cumsum(tri_ref[...], la)
    pm = jnp.where(lane < GLA_RANK, _dt_shift(dt), cum)
    pmt = pm.T
    row = lax.broadcasted_iota(jnp.int32, (c, c), 0)
    col = lax.broadcasted_iota(jnp.int32, (c, c), 1)
    causal = row >= col
    for g0 in range(0, SSD_G, SSD_GB):
        groups = range(g0, g0 + SSD_GB)
        hd_all = range(g0 * SSD_HPG, (g0 + SSD_GB) * SSD_HPG)
        if N_HEADS + g0 < PROJ_SITES:
            next_proj(N_HEADS + g0)
        bms = {g: xbc[:, 1024 + g * SSD_N:1024 + (g + 1) * SSD_N].astype(BF16) for g in groups}
        cms = {g: xbc[:, 1536 + g * SSD_N:1536 + (g + 1) * SSD_N].astype(BF16) for g in groups}
        cbs = {g: _dot_nt(cms[g], bms[g]) for g in groups}
        y_inters = {g: _dot(cms[g], ss_ref[g].astype(BF16)) for g in groups}
        x_hs = {hd: xbc[:, hd * SSD_P:(hd + 1) * SSD_P] for hd in hd_all}
        dt_cols = {hd: pm[:, hd:hd + 1] for hd in hd_all}
        cum_cols = {hd: pm[:, GLA_RANK + hd:GLA_RANK + hd + 1] for hd in hd_all}
        ws = {hd: (cbs[hd // SSD_HPG] * jnp.exp(jnp.where(
            causal, cum_cols[hd] - pmt[GLA_RANK + hd:GLA_RANK + hd + 1, :], NEG_BIG))).astype(BF16)
            for hd in hd_all}
        if N_HEADS + g0 + 1 < PROJ_SITES and SSD_GB > 1:
            next_proj(N_HEADS + g0 + 1)
        y_hs = {hd: _dot(ws[hd], (x_hs[hd] * dt_cols[hd]).astype(BF16)) for hd in hd_all}
        y_parts = {hd: y_hs[hd] + jnp.exp(cum_cols[hd]) * y_inters[hd // SSD_HPG][
            :, (hd % SSD_HPG) * SSD_P:(hd % SSD_HPG + 1) * SSD_P]
            + dskip_ref[:, hd * SSD_P:(hd + 1) * SSD_P] * x_hs[hd] for hd in hd_all}
        lasts = {hd: cum_cols[hd][c - 1:c, :] for hd in hd_all}
        xs_parts = {hd: x_hs[hd] * (dt_cols[hd] * jnp.exp(lasts[hd] - cum_cols[hd])) for hd in hd_all}
        dec_parts = {hd: jnp.broadcast_to(jnp.exp(lasts[hd]), (1, SSD_P)) for hd in hd_all}
        for g in groups:
            hds = range(g * SSD_HPG, (g + 1) * SSD_HPG)
            xs = jnp.concatenate([xs_parts[hd] for hd in hds], axis=1).astype(BF16)
            dec = jnp.concatenate([dec_parts[hd] for hd in hds], axis=1)
            ss_ref[g] = dec * ss_ref[g] + _dot_tn(bms[g], xs)
        yns = {g: _head_norm(jnp.concatenate([y_parts[hd] for hd in range(g * SSD_HPG, (g + 1) * SSD_HPG)], axis=1)
                             * _silu(p_ref[:, 3072 + g * DV:3072 + (g + 1) * DV].astype(F32)),
                             sng_ref[:, g * DV:(g + 1) * DV], False) for g in groups}
        for g in groups:
            z_ref[...] += _dot(yns[g].astype(BF16), wo_ref[D_MODEL + g * DV:D_MODEL + (g + 1) * DV, :])

    o_ref[...] = _layernorm(z_ref[...], lg_ref[...], lb_ref[...])


def _dt_shift(dt):
    return pltpu.roll(dt, LANES - GLA_RANK, 1)


def _layer_cd(x, w_main, w_small, tri, trig, wal, bal, gla_g, conv_w, conv_b, dtb, alog, dskip, ssd_g,
              w_out, ln_g, ln_b, b, s, c, cg):
    nt = s // c
    t, d = x.shape
    n = b * nt
    tok = lambda i: (i, 0)
    nxt = lambda i: (jnp.minimum(i + 1, n - 1), 0)
    full2 = lambda i: (0, 0)
    ch = D_MODEL + 2 * SSD_G * SSD_N
    return pl.pallas_call(
        functools.partial(_layer_cd_kernel, c=c, cg=cg, nt=nt),
        grid=(n,),
        in_specs=[
            pl.BlockSpec((c, d), tok),
            pl.BlockSpec((c, d), nxt),
            pl.BlockSpec((d, P_MAIN), full2),
            pl.BlockSpec((d, LANES), full2),
            pl.BlockSpec((c, c), full2),
            pl.BlockSpec((c, c), full2),
            pl.BlockSpec((LANES, N_HEADS * DK), full2),
            pl.BlockSpec((1, N_HEADS * DK), full2),
            pl.BlockSpec((1, D_MODEL), full2),
            pl.BlockSpec((CONV_K, ch), full2),
            pl.BlockSpec((1, ch), full2),
            pl.BlockSpec((1, LANES), full2),
            pl.BlockSpec((1, LANES), full2),
            pl.BlockSpec((1, D_MODEL), full2),
            pl.BlockSpec((1, D_MODEL), full2),
            pl.BlockSpec((2 * D_MODEL, d), full2),
            pl.BlockSpec((1, d), full2),
            pl.BlockSpec((1, d), full2),
        ],
        out_specs=pl.BlockSpec((c, d), tok),
        out_shape=jax.ShapeDtypeStruct((t, d), F32),
        scratch_shapes=[
            pltpu.VMEM((2, c, P_MAIN), BF16),
            pltpu.VMEM((2, c, LANES), F32),
            pltpu.VMEM((c, d), BF16),
            pltpu.VMEM((c, d), F32),
            pltpu.VMEM((N_HEADS, DV, DK), F32),
            pltpu.VMEM((SSD_G, SSD_N, SSD_HPG * SSD_P), F32),
            pltpu.VMEM((SUBLANES + c, ch), F32),
        ],
        compiler_params=_cparams(("arbitrary",)),
        name="layer_cd",
    )(x, x, w_main, w_small, tri, trig, wal, bal, gla_g, conv_w, conv_b, dtb, alog, dskip, ssd_g, w_out, ln_g, ln_b)


def _split2(x):
    hi = x.astype(BF16)
    return hi, (x - hi.astype(F32)).astype(BF16)


def _router_kernel(h_ref, w_ref, b_ref, ltri_ref, utri_ref, pos_ref, gate_ref, tot_ref, carry_ref, *, cap):
    @pl.when(pl.program_id(0) == 0)
    def _():
        carry_ref[...] = jnp.zeros_like(carry_ref)

    h_hi, h_lo = _split2(h_ref[...])
    w_hi, w_lo = _split2(w_ref[...])
    logits = (_dot(h_hi, w_hi) + _dot(h_lo, w_hi) + _dot(h_hi, w_lo)).T[:N_EXPERTS, :]
    e = jnp.exp(logits - jnp.max(logits, axis=0, keepdims=True))
    probs = e / jnp.sum(e, axis=0, keepdims=True)
    sel = probs + b_ref[...]
    row = lax.broadcasted_iota(jnp.int32, sel.shape, 0)
    pos = row & (EPG - 1)
    gidx = row // EPG

    def member(a, k):
        fwd = pltpu.roll(a, N_EXPERTS - k, 0)
        back = pltpu.roll(a, EPG - k, 0)
        return jnp.where(pos + k < EPG, fwd, back)

    others = [member(sel, k) for k in range(1, EPG)]
    vals = [sel] + others
    top2 = None
    for i in range(EPG):
        for j in range(i + 1, EPG):
            pair = vals[i] + vals[j]
            top2 = pair if top2 is None else jnp.maximum(top2, pair)
    best = jnp.ones(sel.shape, jnp.int32)
    for k in range(1, N_GROUPS):
        other = pltpu.roll(top2, EPG * k, 0)
        wins = jnp.where(gidx >= k, jnp.where(top2 > other, 1, 0), jnp.where(top2 >= other, 1, 0))
        best = best * wins
    rank = jnp.zeros(sel.shape, jnp.int32)
    for k in range(1, EPG):
        v = others[k - 1]
        tie = jnp.where(pos + k >= EPG, 1, 0)
        rank = rank + jnp.where(v > sel, 1, jnp.where(v == sel, tie, 0))
    chosen = jnp.where(best * jnp.where(rank < 2, 1, 0) > 0, 1.0, 0.0)
    gsel = chosen * probs
    comb = gsel / jnp.sum(gsel, axis=0, keepdims=True)

    chosen_b = chosen.astype(BF16)
    lower = _dot(ltri_ref[...], chosen_b)
    first = chosen * jnp.where(lower == 0.0, 1.0, 0.0)
    second = chosen - first
    carry = carry_ref[:, 0:1]
    before = _dot(chosen_b, utri_ref[...])
    slot = carry + before + row.astype(F32) * float(cap)
    pos0 = jnp.sum(first * slot, axis=0, keepdims=True)
    pos1 = jnp.sum(second * slot, axis=0, keepdims=True)
    g0 = jnp.sum(first * comb, axis=0, keepdims=True)
    g1 = jnp.sum(second * comb, axis=0, keepdims=True)
    r8 = lax.broadcasted_iota(jnp.int32, (SUBLANES, sel.shape[1]), 0)
    pos_ref[...] = jnp.where(r8 == 0, pos0, jnp.where(r8 == 1, pos1, 0.0)).astype(jnp.int32)
    gate_ref[...] = jnp.where(r8 == 0, g0, jnp.where(r8 == 1, g1, 0.0))
    total = carry_ref[...] + jnp.sum(chosen, axis=1, keepdims=True)
    carry_ref[...] = total
    tot_ref[...] = total.astype(jnp.int32)


def _router(h, w_pad, b_col, cap, tm=1024):
    t, d = h.shape
    tm = min(tm, t)
    ltri = jnp.asarray(np.tril(np.ones((N_EXPERTS, N_EXPERTS), np.float32), -1), BF16)
    utri = jnp.asarray(np.triu(np.ones((tm, tm), np.float32), 1), BF16)
    return pl.pallas_call(
        functools.partial(_router_kernel, cap=cap),
        grid=(t // tm,),
        in_specs=[
            pl.BlockSpec((tm, d), lambda i: (i, 0)),
            pl.BlockSpec((d, LANES), lambda i: (0, 0)),
            pl.BlockSpec((N_EXPERTS, 1), lambda i: (0, 0)),
            pl.BlockSpec((N_EXPERTS, N_EXPERTS), lambda i: (0, 0)),
            pl.BlockSpec((tm, tm), lambda i: (0, 0)),
        ],
        out_specs=[
            pl.BlockSpec((SUBLANES, tm), lambda i: (0, i)),
            pl.BlockSpec((SUBLANES, tm), lambda i: (0, i)),
            pl.BlockSpec((N_EXPERTS, LANES), lambda i: (0, 0)),
        ],
        out_shape=[
            jax.ShapeDtypeStruct((SUBLANES, t), jnp.int32),
            jax.ShapeDtypeStruct((SUBLANES, t), F32),
            jax.ShapeDtypeStruct((N_EXPERTS, LANES), jnp.int32),
        ],
        scratch_shapes=[pltpu.VMEM((N_EXPERTS, LANES), F32)],
        compiler_params=_cparams(("arbitrary",)),
        name="router",
    )(h, w_pad, b_col, ltri, utri)


def _rebase_kernel(base_ref, pos_ref, o_ref, *, cap):
    p = pos_ref[...]
    e = p // cap
    out = p - e * cap
    for k in range(N_EXPERTS):
        out = out + jnp.where(e == k, base_ref[k], 0)
    o_ref[...] = out


def _rebase(pos, base, cap, tm=8192):
    t = pos.shape[1]
    tm = min(tm, t)
    return pl.pallas_call(
        functools.partial(_rebase_kernel, cap=cap),
        grid_spec=pltpu.PrefetchScalarGridSpec(
            num_scalar_prefetch=1,
            grid=(t // tm,),
            in_specs=[pl.BlockSpec((SUBLANES, tm), lambda i, base: (0, i))],
            out_specs=pl.BlockSpec((SUBLANES, tm), lambda i, base: (0, i)),
        ),
        out_shape=jax.ShapeDtypeStruct(pos.shape, jnp.int32),
        compiler_params=_cparams(("arbitrary",)),
        name="rebase",
    )(base, pos)


def _dispatch_kernel(lo_ref, hi_ref, p0_ref, p1_ref, h_ref, xs_ref, sem, *, tm):
    def issue(i, carry):
        for u in range(SUBLANES):
            r = i * SUBLANES + u
            src = h_ref.at[pl.ds(r, 1), :]
            pltpu.make_async_copy(src, xs_ref.at[pl.ds(p0_ref[r], 1), :], sem).start()
            pltpu.make_async_copy(src, xs_ref.at[pl.ds(p1_ref[r], 1), :], sem).start()
        return carry

    lax.fori_loop(0, tm // SUBLANES, issue, 0)

    @pl.when(pl.program_id(0) == pl.num_programs(0) - 1)
    def _():
        src = h_ref.at[pl.ds(0, 1), :]

        def fill(r, carry):
            pltpu.make_async_copy(src, xs_ref.at[pl.ds(r, 1), :], sem).start()
            return carry

        def drain(r, carry):
            pltpu.make_async_copy(src, xs_ref.at[pl.ds(0, 1), :], sem).wait()
            return carry

        for k in range(N_EXPERTS + 1):
            lax.fori_loop(lo_ref[k], hi_ref[k], fill, 0)
            lax.fori_loop(lo_ref[k], hi_ref[k], drain, 0)

    whole = pltpu.make_async_copy(h_ref, xs_ref.at[pl.ds(0, tm), :], sem)
    whole.wait()
    whole.wait()


def _dispatch(h, p0, p1, fill_lo, fill_hi, n_rows, tm=1024):
    t, d = h.shape
    tm = min(tm, t)
    pos_spec = pl.BlockSpec((tm,), lambda i, lo, hi: (i,), memory_space=pltpu.SMEM)
    return pl.pallas_call(
        functools.partial(_dispatch_kernel, tm=tm),
        grid_spec=pltpu.PrefetchScalarGridSpec(
            num_scalar_prefetch=2,
            grid=(t // tm,),
            in_specs=[pos_spec, pos_spec, pl.BlockSpec((tm, d), lambda i, lo, hi: (i, 0))],
            out_specs=pl.BlockSpec(memory_space=pl.ANY),
            scratch_shapes=[pltpu.SemaphoreType.DMA],
        ),
        out_shape=jax.ShapeDtypeStruct((n_rows, d), F32),
        compiler_params=_cparams(("arbitrary",)),
        name="dispatch",
    )(fill_lo, fill_hi, p0, p1, h)


def _ffn_kernel(exp_ref, x_ref, wg_ref, wu_ref, wd_ref, y_ref, wgb_ref, wub_ref, wdb_ref):
    i = pl.program_id(0)

    @pl.when(jnp.logical_or(i == 0, exp_ref[i] != exp_ref[jnp.maximum(i - 1, 0)]))
    def _():
        wgb_ref[...] = wg_ref[...].astype(BF16)
        wub_ref[...] = wu_ref[...].astype(BF16)
        wdb_ref[...] = wd_ref[...].astype(BF16)

    xb = x_ref[...].astype(BF16)
    act = _silu(_dot(xb, wgb_ref[...])) * _dot(xb, wub_ref[...])
    y_ref[...] = _dot(act.astype(BF16), wdb_ref[...])


def _ffn(xs, tile_exp, wg, wu, wd, layer):
    n, d = xs.shape
    return pl.pallas_call(
        _ffn_kernel,
        grid_spec=pltpu.PrefetchScalarGridSpec(
            num_scalar_prefetch=1,
            grid=(n // TM_F,),
            in_specs=[
                pl.BlockSpec((TM_F, d), lambda i, ex: (i, 0)),
                pl.BlockSpec((None, None, d, D_FF), lambda i, ex: (layer, ex[i], 0, 0)),
                pl.BlockSpec((None, None, d, D_FF), lambda i, ex: (layer, ex[i], 0, 0)),
                pl.BlockSpec((None, None, D_FF, d), lambda i, ex: (layer, ex[i], 0, 0)),
            ],
            out_specs=pl.BlockSpec((TM_F, d), lambda i, ex: (i, 0)),
            scratch_shapes=[pltpu.VMEM((d, D_FF), BF16), pltpu.VMEM((d, D_FF), BF16), pltpu.VMEM((D_FF, d), BF16)],
        ),
        out_shape=jax.ShapeDtypeStruct((n, d), F32),
        compiler_params=_cparams(("arbitrary",)),
        name="expert_ffn",
    )(tile_exp, xs, wg, wu, wd)


def _combine_kernel(p0_ref, p1_ref, gate_ref, h_ref, ys_ref, g_ref, b_ref, o_ref, buf_ref, sem, *, tm):
    def issue(i, carry):
        for u in range(SUBLANES):
            r = i * SUBLANES + u
            pltpu.make_async_copy(ys_ref.at[pl.ds(p0_ref[r], 1), :], buf_ref.at[0, pl.ds(r, 1), :], sem).start()
            pltpu.make_async_copy(ys_ref.at[pl.ds(p1_ref[r], 1), :], buf_ref.at[1, pl.ds(r, 1), :], sem).start()
        return carry

    lax.fori_loop(0, tm // SUBLANES, issue, 0)
    gt = gate_ref[...].T
    for k in range(2):
        pltpu.make_async_copy(ys_ref.at[pl.ds(0, tm), :], buf_ref.at[k], sem).wait()
    z = ALPHA * h_ref[...] + gt[:, 0:1] * buf_ref[0] + gt[:, 1:2] * buf_ref[1]
    o_ref[...] = _layernorm(z, g_ref[...], b_ref[...])


def _combine(h, p0, p1, gates, ys, g, b, tm=1024):
    t, d = h.shape
    tm = min(tm, t)
    return pl.pallas_call(
        functools.partial(_combine_kernel, tm=tm),
        grid=(t // tm,),
        in_specs=[
            pl.BlockSpec((tm,), lambda i: (i,), memory_space=pltpu.SMEM),
            pl.BlockSpec((tm,), lambda i: (i,), memory_space=pltpu.SMEM),
            pl.BlockSpec((SUBLANES, tm), lambda i: (0, i)),
            pl.BlockSpec((tm, d), lambda i: (i, 0)),
            pl.BlockSpec(memory_space=pl.ANY),
            pl.BlockSpec((1, d), lambda i: (0, 0)),
            pl.BlockSpec((1, d), lambda i: (0, 0)),
        ],
        out_specs=pl.BlockSpec((tm, d), lambda i: (i, 0)),
        out_shape=jax.ShapeDtypeStruct((t, d), F32),
        scratch_shapes=[pltpu.VMEM((2, tm, d), F32), pltpu.SemaphoreType.DMA],
        compiler_params=_cparams(("arbitrary",)),
        name="combine_ln",
    )(p0, p1, gates, h, ys, g, b)


def _moe_ln(h, w_pad, b_col, wg, wu, wd, layer, g, b):
    t, d = h.shape
    cap = 2 * t
    nt = (2 * t) // TM_F + N_EXPERTS
    pos, gates, tot = _router(h, w_pad, b_col, cap)
    totals = tot[:, 0]
    ntile = (totals + TM_F - 1) // TM_F
    ends = jnp.cumsum(ntile)
    base = (ends - ntile) * TM_F
    tile_exp = jnp.minimum(jnp.sum((jnp.arange(nt, dtype=jnp.int32)[:, None] >= ends[None, :]).astype(jnp.int32),
                                   axis=1), N_EXPERTS - 1)
    fill_lo = jnp.concatenate([base + totals, ends[-1:] * TM_F]).astype(jnp.int32)
    fill_hi = jnp.concatenate([ends * TM_F, jnp.full((1,), nt * TM_F, jnp.int32)]).astype(jnp.int32)
    pos = _rebase(pos, base.astype(jnp.int32), cap)
    p0, p1 = pos[0], pos[1]
    xs = _dispatch(h, p0, p1, fill_lo, fill_hi, nt * TM_F)
    ys = _ffn(xs, tile_exp.astype(jnp.int32), wg, wu, wd, layer)
    return _combine(h, p0, p1, gates, ys, g, b)


def _retention_tables(c):
    log_g = np.log(1.0 - 2.0 ** (-5.0 - np.arange(N_HEADS, dtype=np.float32))).astype(np.float32)
    idx = np.arange(c, dtype=np.float32)
    diff = idx[:, None] - idx[None, :]
    dmask = np.where(diff >= 0, np.exp(log_g[:, None, None] * np.maximum(diff, 0.0)), 0.0)
    qdec = np.exp(log_g[:, None] * (idx + 1.0))[..., None] * np.ones((1, 1, DK), np.float32)
    kdec = np.exp(log_g[:, None] * (c - 1.0 - idx))[..., None] * np.ones((1, 1, DK), np.float32)
    cdec = np.exp(log_g * c)[:, None, None] * np.ones((1, 1, DV), np.float32)
    return tuple(jnp.asarray(a, F32) for a in (dmask, qdec, kdec, cdec))


def _tri(c, block=None):
    block = c if block is None else block
    return jnp.asarray(np.kron(np.eye(c // block, dtype=np.float32), np.tril(np.ones((block, block), np.float32))),
                       BF16)


def _pad_cols(w, n):
    return jnp.pad(w, ((0, 0), (0, n - w.shape[1])))


def kernel(x, positions, w_in_ab, ret_norm_g, ml_conv_w, ml_conv_b, ml_b_i, ml_b_f, ml_norm_g, w_out_ab,
           w_in_cd, gla_w_alpha, gla_b_alpha, gla_norm_g, ssd_conv_w, ssd_conv_b, ssd_dt_bias, ssd_a_log, ssd_d,
           ssd_norm_g, w_out_cd, w_router, b_router, moe_w_gate, moe_w_up, moe_w_down,
           ln_mix_g, ln_mix_b, ln_ffn_g, ln_ffn_b):
    b, s, d = x.shape
    t = b * s
    c = min(256, s)
    cg = min(64, c)
    h = x.reshape(t, d)
    row = lambda v: v.reshape(1, -1).astype(F32)

    w_pad = _pad_cols(w_router.astype(F32), LANES)
    b_col = b_router.astype(F32).reshape(N_EXPERTS, 1)

    def ffn(hh, layer):
        return _moe_ln(hh, w_pad, b_col, moe_w_gate.astype(F32), moe_w_up.astype(F32), moe_w_down.astype(F32), layer,
                       row(ln_ffn_g[layer]), row(ln_ffn_b[layer]))

    half = np.concatenate([np.arange(0, DK, 2), np.arange(1, DK, 2)])
    perm = np.concatenate([hd * DK + half for hd in range(N_HEADS)])
    w = w_in_ab[0]
    nqk = N_HEADS * DK
    w_main = jnp.concatenate([w[:, :nqk][:, perm], w[:, nqk:2 * nqk][:, perm], w[:, 2 * nqk:P_MAIN]], axis=1)
    inv = ROPE_BASE ** (-jnp.arange(0, DK, 2, dtype=F32) / DK)
    cc, ss = _rope_tables(positions.reshape(t, 1), jnp.concatenate([inv, inv]).reshape(1, DK))
    gate_b = _pad_cols(jnp.concatenate([ml_b_i[0], ml_b_f[0]]).reshape(1, -1).astype(F32), LANES)
    h = _layer_ab(h, w_main.astype(BF16), _pad_cols(w[:, P_MAIN:], LANES).astype(BF16), cc, ss,
                  _retention_tables(c) + (_tri(c),), row(ret_norm_g[0]), row(ml_norm_g[0]),
                  ml_conv_w[0].astype(F32), row(ml_conv_b[0]), gate_b,
                  w_out_ab[0].astype(BF16), row(ln_mix_g[0]), row(ln_mix_b[0]), b, s, c)
    h = ffn(h, 0)

    w = w_in_cd[0]
    g0 = 2 * nqk + 2 * N_HEADS * DV
    s0 = g0 + GLA_RANK
    s1 = s0 + D_MODEL + D_MODEL + 2 * SSD_G * SSD_N
    w_main = jnp.concatenate([w[:, :g0], w[:, s0:s1]], axis=1)
    w_small = _pad_cols(jnp.concatenate([w[:, g0:s0], w[:, s1:]], axis=1), LANES)
    wal = jnp.pad(gla_w_alpha[0], ((0, LANES - GLA_RANK), (0, 0))).astype(BF16)
    lane_pad = lambda v: jnp.pad(v.reshape(1, -1).astype(F32), ((0, 0), (GLA_RANK, LANES - GLA_RANK - SSD_H)))
    dskip = jnp.repeat(ssd_d[0].astype(F32), SSD_P).reshape(1, -1)
    h = _layer_cd(h, w_main.astype(BF16), w_small.astype(BF16), _tri(c), _tri(c, cg), wal, row(gla_b_alpha[0]),
                  row(gla_norm_g[0]), ssd_conv_w[0].astype(F32), row(ssd_conv_b[0]), lane_pad(ssd_dt_bias[0]),
                  lane_pad(ssd_a_log[0]), dskip, row(ssd_norm_g[0]),
                  w_out_cd[0].astype(BF16), row(ln_mix_g[1]), row(ln_mix_b[1]), b, s, c, cg)
    h = ffn(h, 1)
    return h.reshape(b, s, d)
```

```python
import functools
import math

import jax
import jax.numpy as jnp
import numpy as np
from jax import lax
from jax.experimental import pallas as pl
from jax.experimental.pallas import tpu as pltpu

F32 = jnp.float32
BF16 = jnp.bfloat16

D_MODEL = 1024
DEPTH = 2
LN_EPS = 1e-5
NORM_EPS = 1e-6
N_HEADS = 4
DK = D_MODEL // 8
DV = D_MODEL // 4
ROPE_BASE = 10000.0
CONV_K = 4
GLA_RANK = 16
GLA_TAU = 16.0
SSD_P = 64
SSD_H = D_MODEL // SSD_P
SSD_G = 4
SSD_HPG = SSD_H // SSD_G
SSD_N = 128
N_EXPERTS = 16
N_GROUPS = 4
EPG = N_EXPERTS // N_GROUPS
D_FF = D_MODEL // 2
ALPHA = (2.0 * DEPTH) ** 0.25

LANES = 128
SUBLANES = 8
VMEM_LIMIT = 56 * 1024 * 1024

P_MAIN = 6 * D_MODEL
TM_F = 512
SSD_GB = 1
PROJ_PIECE = 256
PROJ_SITES = 7
NEG_BIG = -1e30


def _cparams(sem):
    return pltpu.CompilerParams(dimension_semantics=sem, vmem_limit_bytes=VMEM_LIMIT)


def _dot(a, b):
    return jnp.dot(a, b, preferred_element_type=F32)


def _dot_nt(a, b):
    return lax.dot_general(a, b, (((1,), (1,)), ((), ())), preferred_element_type=F32)


def _dot_tn(a, b):
    return lax.dot_general(a, b, (((0,), (0,)), ((), ())), preferred_element_type=F32)


def _sigmoid(x):
    return 1.0 / (1.0 + jnp.exp(-x))


def _silu(x):
    return x * _sigmoid(x)


def _log_sigmoid(x):
    return jnp.minimum(x, 0.0) - jnp.log(1.0 + jnp.exp(-jnp.abs(x)))


def _softplus(x):
    return jnp.maximum(x, 0.0) + jnp.log(1.0 + jnp.exp(-jnp.abs(x)))


def _tri_cumsum(tri, x):
    hi = x.astype(BF16)
    r1 = x - hi.astype(F32)
    mid = r1.astype(BF16)
    lo = (r1 - mid.astype(F32)).astype(BF16)
    return _dot(tri, hi) + _dot(tri, mid) + _dot(tri, lo)


def _head_norm(y, g, center):
    if center:
        y = y - jnp.mean(y, axis=-1, keepdims=True)
    return y * lax.rsqrt(jnp.mean(y * y, axis=-1, keepdims=True) + NORM_EPS) * g


def _in_projection(x_ref, wm_ref, ws_ref, p_ref, s_ref):
    xb = x_ref[...].astype(BF16)
    for j in range(P_MAIN // D_MODEL):
        sl = slice(j * D_MODEL, (j + 1) * D_MODEL)
        p_ref[:, sl] = _dot(xb, wm_ref[:, sl]).astype(BF16)
    s_ref[...] = _dot(xb, ws_ref[...])


def _pipelined_in_projection(x_ref, xn_ref, wm_ref, ws_ref, pbuf_ref, sbuf_ref, xb_ref):
    step = pl.program_id(0)
    slot = step % 2

    @pl.when(step == 0)
    def _():
        _in_projection(x_ref, wm_ref, ws_ref, pbuf_ref.at[0], sbuf_ref.at[0])

    nxt_p = pbuf_ref.at[1 - slot]
    nxt_s = sbuf_ref.at[1 - slot]
    xb_ref[...] = xn_ref[...].astype(BF16)

    def main_piece(j):
        def run():
            sl = slice(j * PROJ_PIECE, (j + 1) * PROJ_PIECE)
            nxt_p[:, sl] = _dot(xb_ref[...], wm_ref[:, sl]).astype(BF16)
        return run

    def small_piece():
        nxt_s[...] = _dot(xb_ref[...], ws_ref[...])

    thunks = [main_piece(j) for j in range(P_MAIN // PROJ_PIECE)] + [small_piece]

    def advance(site):
        for k in range(site * len(thunks) // PROJ_SITES, (site + 1) * len(thunks) // PROJ_SITES):
            thunks[k]()

    return pbuf_ref.at[slot], sbuf_ref[slot], advance


def _layernorm(z, g, b):
    zc = z - jnp.mean(z, axis=-1, keepdims=True)
    return zc * lax.rsqrt(jnp.mean(zc * zc, axis=-1, keepdims=True) + LN_EPS) * g + b


def _rope_kernel(pos_ref, inv_ref, cc_ref, ss_ref):
    half = pos_ref.shape[0] // 2
    lane = lax.broadcasted_iota(jnp.int32, (half, DK), 1)
    low = lane < DK // 2
    ang = jnp.where(low, pos_ref[0:half, :], pos_ref[half:2 * half, :]).astype(F32) * inv_ref[...]
    c = jnp.cos(ang)
    s = jnp.sin(ang)
    c_sw = pltpu.roll(c, DK // 2, 1)
    s_sw = pltpu.roll(s, DK // 2, 1)
    cc_ref[0:half, :] = jnp.where(low, c, c_sw)
    cc_ref[half:2 * half, :] = jnp.where(low, c_sw, c)
    ss_ref[0:half, :] = jnp.where(low, -s, s_sw)
    ss_ref[half:2 * half, :] = jnp.where(low, -s_sw, s)


def _rope_tables(pos_col, inv2, tm=2048):
    t = pos_col.shape[0]
    tm = min(tm, t)
    return pl.pallas_call(
        _rope_kernel,
        grid=(t // tm,),
        in_specs=[pl.BlockSpec((tm, 1), lambda i: (i, 0)), pl.BlockSpec((1, DK), lambda i: (0, 0))],
        out_specs=[pl.BlockSpec((tm, DK), lambda i: (i, 0))] * 2,
        out_shape=[jax.ShapeDtypeStruct((t, DK), F32)] * 2,
        compiler_params=_cparams(("parallel",)),
        name="rope_tables",
    )(pos_col, inv2)


def _causal_conv_tile(hist_ref, u, w_ref, b_ref, c):
    hist_ref[pl.ds(SUBLANES, c), :] = u
    acc = b_ref[...] + w_ref[CONV_K - 1:CONV_K, :] * u
    for j in range(CONV_K - 1):
        acc = acc + w_ref[j:j + 1, :] * hist_ref[pl.ds(SUBLANES - (CONV_K - 1) + j, c), :]
    hist_ref[pl.ds(0, SUBLANES), :] = hist_ref[pl.ds(c, SUBLANES), :]
    return acc


def _layer_ab_kernel(x_ref, xn_ref, wm_ref, ws_ref, cc_ref, ss_ref, dmask_ref, qdec_ref, kdec_ref, cdec_ref,
                     tri_ref, rng_ref, mng_ref, cw_ref, cb_ref, gb_ref, wo_ref, lg_ref, lb_ref, o_ref,
                     pbuf_ref, sbuf_ref, xb_ref, z_ref, rs_ref, mc_ref, mn_ref, mm_ref, hist_ref, *, c, nt):
    @pl.when(pl.program_id(0) % nt == 0)
    def _():
        rs_ref[...] = jnp.zeros_like(rs_ref)
        mc_ref[...] = jnp.zeros_like(mc_ref)
        mn_ref[...] = jnp.zeros_like(mn_ref)
        mm_ref[...] = jnp.zeros_like(mm_ref)
        hist_ref[pl.ds(0, SUBLANES), :] = jnp.zeros((SUBLANES, hist_ref.shape[1]), F32)

    p_ref, small, next_proj = _pipelined_in_projection(x_ref, xn_ref, wm_ref, ws_ref, pbuf_ref, sbuf_ref, xb_ref)
    z_ref[...] = ALPHA * x_ref[...]
    cc = cc_ref[...]
    ss = ss_ref[...]
    kscale = DK ** -0.5

    def rot(t):
        return t * cc + pltpu.roll(t, DK // 2, 1) * ss

    heads = range(N_HEADS)
    qs = [rot(p_ref[:, h * DK:(h + 1) * DK].astype(F32)) for h in heads]
    ks = [rot(p_ref[:, 512 + h * DK:512 + (h + 1) * DK].astype(F32)) * kscale for h in heads]
    vs = [p_ref[:, 1024 + h * DV:1024 + (h + 1) * DV] for h in heads]
    next_proj(0)
    scs = [_dot_nt(qs[h].astype(BF16), ks[h].astype(BF16)) * dmask_ref[h] for h in heads]
    next_proj(1)
    ys = [_dot(scs[h].astype(BF16), vs[h]) + _dot((qs[h] * qdec_ref[h]).astype(BF16), rs_ref[h].astype(BF16))
          for h in heads]
    next_proj(2)
    for h in heads:
        rs_ref[h] = cdec_ref[h] * rs_ref[h] + _dot_tn((ks[h] * kdec_ref[h]).astype(BF16), vs[h])
    yns = [_head_norm(ys[h], rng_ref[:, h * DV:(h + 1) * DV], True)
           * _silu(p_ref[:, 2048 + h * DV:2048 + (h + 1) * DV].astype(F32)) for h in heads]
    next_proj(3)
    for h in heads:
        z_ref[...] += _dot(yns[h].astype(BF16), wo_ref[h * DV:(h + 1) * DV, :])

    u = p_ref[:, 3072:4096].astype(F32)
    qk = _silu(_causal_conv_tile(hist_ref, u, cw_ref, cb_ref, c))
    gates = small + gb_ref[...]
    lane = lax.broadcasted_iota(jnp.int32, gates.shape, 1)
    is_f = (lane // N_HEADS) == 1
    logf = jnp.where(is_f, _log_sigmoid(gates), 0.0)
    bcum = _tri_cumsum(tri_ref[...], logf)
    pm = jnp.where(lane < N_HEADS, gates, bcum)
    pmt = pm.T
    row = lax.broadcasted_iota(jnp.int32, (c, c), 0)
    col = lax.broadcasted_iota(jnp.int32, (c, c), 1)
    causal = row >= col
    qs = [qk[:, h * DK:(h + 1) * DK] for h in heads]
    ks = [qk[:, 512 + h * DK:512 + (h + 1) * DK] * kscale for h in heads]
    vs = [p_ref[:, 4096 + h * DV:4096 + (h + 1) * DV] for h in heads]
    i_cols = [pm[:, h:h + 1] for h in heads]
    b_cols = [pm[:, N_HEADS + h:N_HEADS + h + 1] for h in heads]
    m_prevs = [mm_ref[h][:, 0:1] for h in heads]
    a_s = [b_cols[h] + m_prevs[h] for h in heads]
    ds = [jnp.where(causal, b_cols[h] - pmt[N_HEADS + h:N_HEADS + h + 1, :] + pmt[h:h + 1, :], NEG_BIG)
          for h in heads]
    m_ts = [jnp.maximum(a_s[h], jnp.max(ds[h], axis=1, keepdims=True)) for h in heads]
    w_inters = [jnp.exp(a_s[h] - m_ts[h]) for h in heads]
    next_proj(4)
    s_qks = [_dot_nt(qs[h].astype(BF16), ks[h].astype(BF16)) * jnp.exp(ds[h] - m_ts[h]) for h in heads]
    next_proj(5)
    nums = [_dot(s_qks[h].astype(BF16), vs[h]) + w_inters[h] * _dot(qs[h].astype(BF16), mc_ref[h].astype(BF16))
            for h in heads]
    dens = [jnp.sum(s_qks[h], axis=1, keepdims=True)
            + w_inters[h] * jnp.sum(qs[h] * mn_ref[h], axis=1, keepdims=True) for h in heads]
    hhs = [nums[h] / jnp.maximum(jnp.abs(dens[h]), jnp.exp(-m_ts[h])) for h in heads]
    next_proj(6)
    for h in heads:
        b_tot = b_cols[h][c - 1:c, :]
        lw = b_tot - b_cols[h] + i_cols[h]
        m_new = jnp.maximum(b_tot + m_prevs[h], jnp.max(lw, axis=0, keepdims=True))
        kd = ks[h] * jnp.exp(lw - m_new)
        decay = jnp.exp(b_tot + m_prevs[h] - m_new)
        mc_ref[h] = decay * mc_ref[h] + _dot_tn(kd.astype(BF16), vs[h])
        mn_ref[h] = decay * mn_ref[h] + jnp.sum(kd, axis=0, keepdims=True)
        mm_ref[h] = jnp.broadcast_to(m_new, (1, LANES))
    yns = [_head_norm(hhs[h] * _sigmoid(p_ref[:, 5120 + h * DV:5120 + (h + 1) * DV].astype(F32)),
                      mng_ref[:, h * DV:(h + 1) * DV], True) for h in heads]
    for h in heads:
        z_ref[...] += _dot(yns[h].astype(BF16), wo_ref[D_MODEL + h * DV:D_MODEL + (h + 1) * DV, :])

    o_ref[...] = _layernorm(z_ref[...], lg_ref[...], lb_ref[...])


def _layer_ab(x, w_main, w_small, cc, ss, tabs, ret_g, ml_g, conv_w, conv_b, gate_b, w_out, ln_g, ln_b, b, s, c):
    nt = s // c
    t, d = x.shape
    n = b * nt
    tok = lambda i: (i, 0)
    nxt = lambda i: (jnp.minimum(i + 1, n - 1), 0)
    full2 = lambda i: (0, 0)
    full3 = lambda i: (0, 0, 0)
    dmask, qdec, kdec, cdec, tri = tabs
    return pl.pallas_call(
        functools.partial(_layer_ab_kernel, c=c, nt=nt),
        grid=(n,),
        in_specs=[
            pl.BlockSpec((c, d), tok),
            pl.BlockSpec((c, d), nxt),
            pl.BlockSpec((d, P_MAIN), full2),
            pl.BlockSpec((d, LANES), full2),
            pl.BlockSpec((c, DK), tok),
            pl.BlockSpec((c, DK), tok),
            pl.BlockSpec((N_HEADS, c, c), full3),
            pl.BlockSpec((N_HEADS, c, DK), full3),
            pl.BlockSpec((N_HEADS, c, DK), full3),
            pl.BlockSpec((N_HEADS, 1, DV), full3),
            pl.BlockSpec((c, c), full2),
            pl.BlockSpec((1, D_MODEL), full2),
            pl.BlockSpec((1, D_MODEL), full2),
            pl.BlockSpec((CONV_K, D_MODEL), full2),
            pl.BlockSpec((1, D_MODEL), full2),
            pl.BlockSpec((1, LANES), full2),
            pl.BlockSpec((2 * D_MODEL, d), full2),
            pl.BlockSpec((1, d), full2),
            pl.BlockSpec((1, d), full2),
        ],
        out_specs=pl.BlockSpec((c, d), tok),
        out_shape=jax.ShapeDtypeStruct((t, d), F32),
        scratch_shapes=[
            pltpu.VMEM((2, c, P_MAIN), BF16),
            pltpu.VMEM((2, c, LANES), F32),
            pltpu.VMEM((c, d), BF16),
            pltpu.VMEM((c, d), F32),
            pltpu.VMEM((N_HEADS, DK, DV), F32),
            pltpu.VMEM((N_HEADS, DK, DV), F32),
            pltpu.VMEM((N_HEADS, 1, DK), F32),
            pltpu.VMEM((N_HEADS, 1, LANES), F32),
            pltpu.VMEM((SUBLANES + c, D_MODEL), F32),
        ],
        compiler_params=_cparams(("arbitrary",)),
        name="layer_ab",
    )(x, x, w_main, w_small, cc, ss, dmask, qdec, kdec, cdec, tri, ret_g, ml_g, conv_w, conv_b, gate_b,
      w_out, ln_g, ln_b)


def _layer_cd_kernel(x_ref, xn_ref, wm_ref, ws_ref, tri_ref, trig_ref, wal_ref, bal_ref, gng_ref, cw_ref, cb_ref,
                     dtb_ref, alog_ref, dskip_ref, sng_ref, wo_ref, lg_ref, lb_ref, o_ref,
                     pbuf_ref, sbuf_ref, xb_ref, z_ref, gs_ref, ss_ref, hist_ref, *, c, cg, nt):
    @pl.when(pl.program_id(0) % nt == 0)
    def _():
        gs_ref[...] = jnp.zeros_like(gs_ref)
        ss_ref[...] = jnp.zeros_like(ss_ref)
        hist_ref[pl.ds(0, SUBLANES), :] = jnp.zeros((SUBLANES, hist_ref.shape[1]), F32)

    p_ref, small, next_proj = _pipelined_in_projection(x_ref, xn_ref, wm_ref, ws_ref, pbuf_ref, sbuf_ref, xb_ref)
    z_ref[...] = ALPHA * x_ref[...]

    log_alpha = _log_sigmoid(_dot(small.astype(BF16), wal_ref[...]) + bal_ref[...]) * (1.0 / GLA_TAU)
    qscale = DK ** -0.5
    trig = trig_ref[...]
    rowg = lax.broadcasted_iota(jnp.int32, (cg, cg), 0)
    colg = lax.broadcasted_iota(jnp.int32, (cg, cg), 1)
    causal_g = rowg >= colg
    heads = range(N_HEADS)
    subs = range(c // cg)
    rows = lambda j: slice(j * cg, (j + 1) * cg)
    cols = lambda h: slice(h * DK, (h + 1) * DK)
    bcum = _tri_cumsum(trig, log_alpha)
    b_tots = [bcum[(j + 1) * cg - 1:(j + 1) * cg, :] for j in subs]
    e_end = jnp.concatenate([jnp.exp(b_tots[j] - bcum[rows(j), :]) for j in subs], axis=0)
    next_proj(0)
    q_all = p_ref[:, 0:N_HEADS * DK].astype(F32) * qscale
    k_all = p_ref[:, N_HEADS * DK:2 * N_HEADS * DK].astype(F32)
    qe = (q_all * jnp.exp(bcum)).astype(BF16)
    ke = (k_all * jnp.exp(-bcum)).astype(BF16)
    kd = (k_all * e_end).astype(BF16)
    vs = [[p_ref[rows(j), 1024 + h * DV:1024 + (h + 1) * DV] for h in heads] for j in subs]
    next_proj(1)
    scs = [[jnp.where(causal_g, _dot_nt(qe[rows(j), cols(h)], ke[rows(j), cols(h)]), 0.0).astype(BF16)
            for h in heads] for j in subs]
    next_proj(2)
    y_in = [[_dot(scs[j][h], vs[j][h]) for h in heads] for j in subs]
    next_proj(3)
    sts = [gs_ref[h] for h in heads]
    ys = [[] for _ in heads]
    for j in subs:
        for h in heads:
            ys[h].append(y_in[j][h] + _dot_nt(qe[rows(j), cols(h)], sts[h].astype(BF16)))
        e_tot = jnp.exp(b_tots[j])
        sts = [e_tot[:, cols(h)] * sts[h] + _dot_tn(vs[j][h], kd[rows(j), cols(h)]) for h in heads]
    for h in heads:
        gs_ref[h] = sts[h]
    yns = [_head_norm(jnp.concatenate(ys[h], axis=0), gng_ref[:, h * DV:(h + 1) * DV], False)
           * _silu(p_ref[:, 2048 + h * DV:2048 + (h + 1) * DV].astype(F32)) for h in heads]
    z_ref[...] += _dot(jnp.concatenate([yns[h].astype(BF16) for h in heads], axis=1), wo_ref[0:D_MODEL, :])

    u = p_ref[:, 4096:6144].astype(F32)
    xbc = _silu(_causal_conv_tile(hist_ref, u, cw_ref, cb_ref, c))
    lane = lax.broadcasted_iota(jnp.int32, small.shape, 1)
    is_dt = (lane // GLA_RANK) == 1
    dt = jnp.where(is_dt, _softplus(small + dtb_ref[...]), 0.0)
    la = dt * -jnp.exp(alog_ref[...])
    cum = _tri_cumsum(tri_ref[...], la)
    pm = jnp.where(lane < GLA_RANK, _dt_shift(dt), cum)
    pmt = pm.T
    row = lax.broadcasted_iota(jnp.int32, (c, c), 0)
    col = lax.broadcasted_iota(jnp.int32, (c, c), 1)
    causal = row >= col
    for g0 in range(0, SSD_G, SSD_GB):
        groups = range(g0, g0 + SSD_GB)
        hd_all = range(g0 * SSD_HPG, (g0 + SSD_GB) * SSD_HPG)
        if N_HEADS + g0 < PROJ_SITES:
            next_proj(N_HEADS + g0)
        bms = {g: xbc[:, 1024 + g * SSD_N:1024 + (g + 1) * SSD_N].astype(BF16) for g in groups}
        cms = {g: xbc[:, 1536 + g * SSD_N:1536 + (g + 1) * SSD_N].astype(BF16) for g in groups}
        cbs = {g: _dot_nt(cms[g], bms[g]) for g in groups}
        y_inters = {g: _dot(cms[g], ss_ref[g].astype(BF16)) for g in groups}
        x_hs = {hd: xbc[:, hd * SSD_P:(hd + 1) * SSD_P] for hd in hd_all}
        dt_cols = {hd: pm[:, hd:hd + 1] for hd in hd_all}
        cum_cols = {hd: pm[:, GLA_RANK + hd:GLA_RANK + hd + 1] for hd in hd_all}
        ws = {hd: (cbs[hd // SSD_HPG] * jnp.exp(jnp.where(
            causal, cum_cols[hd] - pmt[GLA_RANK + hd:GLA_RANK + hd + 1, :], NEG_BIG))).astype(BF16)
            for hd in hd_all}
        if N_HEADS + g0 + 1 < PROJ_SITES and SSD_GB > 1:
            next_proj(N_HEADS + g0 + 1)
        y_hs = {hd: _dot(ws[hd], (x_hs[hd] * dt_cols[hd]).astype(BF16)) for hd in hd_all}
        y_parts = {hd: y_hs[hd] + jnp.exp(cum_cols[hd]) * y_inters[hd // SSD_HPG][
            :, (hd % SSD_HPG) * SSD_P:(hd % SSD_HPG + 1) * SSD_P]
            + dskip_ref[:, hd * SSD_P:(hd + 1) * SSD_P] * x_hs[hd] for hd in hd_all}
        lasts = {hd: cum_cols[hd][c - 1:c, :] for hd in hd_all}
        xs_parts = {hd: x_hs[hd] * (dt_cols[hd] * jnp.exp(lasts[hd] - cum_cols[hd])) for hd in hd_all}
        dec_parts = {hd: jnp.broadcast_to(jnp.exp(lasts[hd]), (1, SSD_P)) for hd in hd_all}
        for g in groups:
            hds = range(g * SSD_HPG, (g + 1) * SSD_HPG)
            xs = jnp.concatenate([xs_parts[hd] for hd in hds], axis=1).astype(BF16)
            dec = jnp.concatenate([dec_parts[hd] for hd in hds], axis=1)
            ss_ref[g] = dec * ss_ref[g] + _dot_tn(bms[g], xs)
        yns = {g: _head_norm(jnp.concatenate([y_parts[hd] for hd in range(g * SSD_HPG, (g + 1) * SSD_HPG)], axis=1)
                             * _silu(p_ref[:, 3072 + g * DV:3072 + (g + 1) * DV].astype(F32)),
                             sng_ref[:, g * DV:(g + 1) * DV], False) for g in groups}
        for g in groups:
            z_ref[...] += _dot(yns[g].astype(BF16), wo_ref[D_MODEL + g * DV:D_MODEL + (g + 1) * DV, :])

    o_ref[...] = _layernorm(z_ref[...], lg_ref[...], lb_ref[...])


def _dt_shift(dt):
    return pltpu.roll(dt, LANES - GLA_RANK, 1)


def _layer_cd(x, w_main, w_small, tri, trig, wal, bal, gla_g, conv_w, conv_b, dtb, alog, dskip, ssd_g,
              w_out, ln_g, ln_b, b, s, c, cg):
    nt = s // c
    t, d = x.shape
    n = b * nt
    tok = lambda i: (i, 0)
    nxt = lambda i: (jnp.minimum(i + 1, n - 1), 0)
    full2 = lambda i: (0, 0)
    ch = D_MODEL + 2 * SSD_G * SSD_N
    return pl.pallas_call(
        functools.partial(_layer_cd_kernel, c=c, cg=cg, nt=nt),
        grid=(n,),
        in_specs=[
            pl.BlockSpec((c, d), tok),
            pl.BlockSpec((c, d), nxt),
            pl.BlockSpec((d, P_MAIN), full2),
            pl.BlockSpec((d, LANES), full2),
            pl.BlockSpec((c, c), full2),
            pl.BlockSpec((c, c), full2),
            pl.BlockSpec((LANES, N_HEADS * DK), full2),
            pl.BlockSpec((1, N_HEADS * DK), full2),
            pl.BlockSpec((1, D_MODEL), full2),
            pl.BlockSpec((CONV_K, ch), full2),
            pl.BlockSpec((1, ch), full2),
            pl.BlockSpec((1, LANES), full2),
            pl.BlockSpec((1, LANES), full2),
            pl.BlockSpec((1, D_MODEL), full2),
            pl.BlockSpec((1, D_MODEL), full2),
            pl.BlockSpec((2 * D_MODEL, d), full2),
            pl.BlockSpec((1, d), full2),
            pl.BlockSpec((1, d), full2),
        ],
        out_specs=pl.BlockSpec((c, d), tok),
        out_shape=jax.ShapeDtypeStruct((t, d), F32),
        scratch_shapes=[
            pltpu.VMEM((2, c, P_MAIN), BF16),
            pltpu.VMEM((2, c, LANES), F32),
            pltpu.VMEM((c, d), BF16),
            pltpu.VMEM((c, d), F32),
            pltpu.VMEM((N_HEADS, DV, DK), F32),
            pltpu.VMEM((SSD_G, SSD_N, SSD_HPG * SSD_P), F32),
            pltpu.VMEM((SUBLANES + c, ch), F32),
        ],
        compiler_params=_cparams(("arbitrary",)),
        name="layer_cd",
    )(x, x, w_main, w_small, tri, trig, wal, bal, gla_g, conv_w, conv_b, dtb, alog, dskip, ssd_g, w_out, ln_g, ln_b)


def _split2(x):
    hi = x.astype(BF16)
    return hi, (x - hi.astype(F32)).astype(BF16)


def _router_kernel(h_ref, w_ref, b_ref, ltri_ref, utri_ref, pos_ref, gate_ref, tot_ref, carry_ref, *, cap):
    @pl.when(pl.program_id(0) == 0)
    def _():
        carry_ref[...] = jnp.zeros_like(carry_ref)

    h_hi, h_lo = _split2(h_ref[...])
    w_hi, w_lo = _split2(w_ref[...])
    logits = (_dot(h_hi, w_hi) + _dot(h_lo, w_hi) + _dot(h_hi, w_lo)).T[:N_EXPERTS, :]
    e = jnp.exp(logits - jnp.max(logits, axis=0, keepdims=True))
    probs = e / jnp.sum(e, axis=0, keepdims=True)
    sel = probs + b_ref[...]
    row = lax.broadcasted_iota(jnp.int32, sel.shape, 0)
    pos = row & (EPG - 1)
    gidx = row // EPG

    def member(a, k):
        fwd = pltpu.roll(a, N_EXPERTS - k, 0)
        back = pltpu.roll(a, EPG - k, 0)
        return jnp.where(pos + k < EPG, fwd, back)

    others = [member(sel, k) for k in range(1, EPG)]
    vals = [sel] + others
    top2 = None
    for i in range(EPG):
        for j in range(i + 1, EPG):
            pair = vals[i] + vals[j]
            top2 = pair if top2 is None else jnp.maximum(top2, pair)
    best = jnp.ones(sel.shape, jnp.int32)
    for k in range(1, N_GROUPS):
        other = pltpu.roll(top2, EPG * k, 0)
        wins = jnp.where(gidx >= k, jnp.where(top2 > other, 1, 0), jnp.where(top2 >= other, 1, 0))
        best = best * wins
    rank = jnp.zeros(sel.shape, jnp.int32)
    for k in range(1, EPG):
        v = others[k - 1]
        tie = jnp.where(pos + k >= EPG, 1, 0)
        rank = rank + jnp.where(v > sel, 1, jnp.where(v == sel, tie, 0))
    chosen = jnp.where(best * jnp.where(rank < 2, 1, 0) > 0, 1.0, 0.0)
    gsel = chosen * probs
    comb = gsel / jnp.sum(gsel, axis=0, keepdims=True)

    chosen_b = chosen.astype(BF16)
    lower = _dot(ltri_ref[...], chosen_b)
    first = chosen * jnp.where(lower == 0.0, 1.0, 0.0)
    second = chosen - first
    run = carry_ref[:, 0:1]
    parts = []
    for blk in range(sel.shape[1] // LANES):
        sl = slice(blk * LANES, (blk + 1) * LANES)
        parts.append(run + _dot(chosen_b[:, sl], utri_ref[...]))
        run = run + jnp.sum(chosen[:, sl], axis=1, keepdims=True)
    slot = jnp.concatenate(parts, axis=1) + row.astype(F32) * float(cap)
    pos0 = jnp.sum(first * slot, axis=0, keepdims=True)
    pos1 = jnp.sum(second * slot, axis=0, keepdims=True)
    g0 = jnp.sum(first * comb, axis=0, keepdims=True)
    g1 = jnp.sum(second * comb, axis=0, keepdims=True)
    r8 = lax.broadcasted_iota(jnp.int32, (SUBLANES, sel.shape[1]), 0)
    pos_ref[...] = jnp.where(r8 == 0, pos0, jnp.where(r8 == 1, pos1, 0.0)).astype(jnp.int32)
    gate_ref[...] = jnp.where(r8 == 0, g0, jnp.where(r8 == 1, g1, 0.0))
    total = jnp.broadcast_to(run, carry_ref.shape)
    carry_ref[...] = total
    tot_ref[...] = total.astype(jnp.int32)


def _router(h, w_pad, b_col, cap, tm=1024):
    t, d = h.shape
    tm = min(tm, t)
    ltri = jnp.asarray(np.tril(np.ones((N_EXPERTS, N_EXPERTS), np.float32), -1), BF16)
    utri = jnp.asarray(np.triu(np.ones((LANES, LANES), np.float32), 1), BF16)
    return pl.pallas_call(
        functools.partial(_router_kernel, cap=cap),
        grid=(t // tm,),
        in_specs=[
            pl.BlockSpec((tm, d), lambda i: (i, 0)),
            pl.BlockSpec((d, LANES), lambda i: (0, 0)),
            pl.BlockSpec((N_EXPERTS, 1), lambda i: (0, 0)),
            pl.BlockSpec((N_EXPERTS, N_EXPERTS), lambda i: (0, 0)),
            pl.BlockSpec((LANES, LANES), lambda i: (0, 0)),
        ],
        out_specs=[
            pl.BlockSpec((SUBLANES, tm), lambda i: (0, i)),
            pl.BlockSpec((SUBLANES, tm), lambda i: (0, i)),
            pl.BlockSpec((N_EXPERTS, LANES), lambda i: (0, 0)),
        ],
        out_shape=[
            jax.ShapeDtypeStruct((SUBLANES, t), jnp.int32),
            jax.ShapeDtypeStruct((SUBLANES, t), F32),
            jax.ShapeDtypeStruct((N_EXPERTS, LANES), jnp.int32),
        ],
        scratch_shapes=[pltpu.VMEM((N_EXPERTS, LANES), F32)],
        compiler_params=_cparams(("arbitrary",)),
        name="router",
    )(h, w_pad, b_col, ltri, utri)


def _rebase_kernel(base_ref, pos_ref, o_ref, *, cap):
    p = pos_ref[...]
    e = p // cap
    out = p - e * cap
    for k in range(N_EXPERTS):
        out = out + jnp.where(e == k, base_ref[k], 0)
    o_ref[...] = out


def _rebase(pos, base, cap, tm=8192):
    t = pos.shape[1]
    tm = min(tm, t)
    return pl.pallas_call(
        functools.partial(_rebase_kernel, cap=cap),
        grid_spec=pltpu.PrefetchScalarGridSpec(
            num_scalar_prefetch=1,
            grid=(t // tm,),
            in_specs=[pl.BlockSpec((SUBLANES, tm), lambda i, base: (0, i))],
            out_specs=pl.BlockSpec((SUBLANES, tm), lambda i, base: (0, i)),
        ),
        out_shape=jax.ShapeDtypeStruct(pos.shape, jnp.int32),
        compiler_params=_cparams(("arbitrary",)),
        name="rebase",
    )(base, pos)


def _dispatch_kernel(lo_ref, hi_ref, p0_ref, p1_ref, h_ref, xs_ref, sem, *, tm):
    def issue(i, carry):
        for u in range(SUBLANES):
            r = i * SUBLANES + u
            src = h_ref.at[pl.ds(r, 1), :]
            pltpu.make_async_copy(src, xs_ref.at[pl.ds(p0_ref[r], 1), :], sem).start(priority=0)
            pltpu.make_async_copy(src, xs_ref.at[pl.ds(p1_ref[r], 1), :], sem).start(priority=1)
        return carry

    lax.fori_loop(0, tm // SUBLANES, issue, 0)

    @pl.when(pl.program_id(0) == pl.num_programs(0) - 1)
    def _():
        src = h_ref.at[pl.ds(0, 1), :]

        def fill(r, carry):
            pltpu.make_async_copy(src, xs_ref.at[pl.ds(r, 1), :], sem).start()
            return carry

        def drain(r, carry):
            pltpu.make_async_copy(src, xs_ref.at[pl.ds(0, 1), :], sem).wait()
            return carry

        for k in range(N_EXPERTS + 1):
            lax.fori_loop(lo_ref[k], hi_ref[k], fill, 0)
            lax.fori_loop(lo_ref[k], hi_ref[k], drain, 0)

    whole = pltpu.make_async_copy(h_ref, xs_ref.at[pl.ds(0, tm), :], sem)
    whole.wait()
    whole.wait()


def _dispatch(h, p0, p1, fill_lo, fill_hi, n_rows, tm=1024):
    t, d = h.shape
    tm = min(tm, t)
    pos_spec = pl.BlockSpec((tm,), lambda i, lo, hi: (i,), memory_space=pltpu.SMEM)
    return pl.pallas_call(
        functools.partial(_dispatch_kernel, tm=tm),
        grid_spec=pltpu.PrefetchScalarGridSpec(
            num_scalar_prefetch=2,
            grid=(t // tm,),
            in_specs=[pos_spec, pos_spec, pl.BlockSpec((tm, d), lambda i, lo, hi: (i, 0))],
            out_specs=pl.BlockSpec(memory_space=pl.ANY),
            scratch_shapes=[pltpu.SemaphoreType.DMA],
        ),
        out_shape=jax.ShapeDtypeStruct((n_rows, d), F32),
        compiler_params=_cparams(("arbitrary",)),
        name="dispatch",
    )(fill_lo, fill_hi, p0, p1, h)


def _ffn_kernel(exp_ref, x_ref, wg_ref, wu_ref, wd_ref, y_ref, wgb_ref, wub_ref, wdb_ref):
    i = pl.program_id(0)

    @pl.when(jnp.logical_or(i == 0, exp_ref[i] != exp_ref[jnp.maximum(i - 1, 0)]))
    def _():
        wgb_ref[...] = wg_ref[...].astype(BF16)
        wub_ref[...] = wu_ref[...].astype(BF16)
        wdb_ref[...] = wd_ref[...].astype(BF16)

    xb = x_ref[...].astype(BF16)
    act = _silu(_dot(xb, wgb_ref[...])) * _dot(xb, wub_ref[...])
    y_ref[...] = _dot(act.astype(BF16), wdb_ref[...])


def _ffn(xs, tile_exp, wg, wu, wd, layer):
    n, d = xs.shape
    return pl.pallas_call(
        _ffn_kernel,
        grid_spec=pltpu.PrefetchScalarGridSpec(
            num_scalar_prefetch=1,
            grid=(n // TM_F,),
            in_specs=[
                pl.BlockSpec((TM_F, d), lambda i, ex: (i, 0)),
                pl.BlockSpec((None, None, d, D_FF), lambda i, ex: (layer, ex[i], 0, 0)),
                pl.BlockSpec((None, None, d, D_FF), lambda i, ex: (layer, ex[i], 0, 0)),
                pl.BlockSpec((None, None, D_FF, d), lambda i, ex: (layer, ex[i], 0, 0)),
            ],
            out_specs=pl.BlockSpec((TM_F, d), lambda i, ex: (i, 0)),
            scratch_shapes=[pltpu.VMEM((d, D_FF), BF16), pltpu.VMEM((d, D_FF), BF16), pltpu.VMEM((D_FF, d), BF16)],
        ),
        out_shape=jax.ShapeDtypeStruct((n, d), F32),
        compiler_params=_cparams(("arbitrary",)),
        name="expert_ffn",
    )(tile_exp, xs, wg, wu, wd)


def _combine_kernel(p0_ref, p1_ref, gate_ref, h_ref, ys_ref, g_ref, b_ref, o_ref, buf_ref, sem, *, tm):
    def issue(i, carry):
        for u in range(SUBLANES):
            r = i * SUBLANES + u
            pltpu.make_async_copy(ys_ref.at[pl.ds(p0_ref[r], 1), :], buf_ref.at[0, pl.ds(r, 1), :],
                                  sem).start(priority=0)
            pltpu.make_async_copy(ys_ref.at[pl.ds(p1_ref[r], 1), :], buf_ref.at[1, pl.ds(r, 1), :],
                                  sem).start(priority=1)
        return carry

    lax.fori_loop(0, tm // SUBLANES, issue, 0)
    gt = gate_ref[...].T
    for k in range(2):
        pltpu.make_async_copy(ys_ref.at[pl.ds(0, tm), :], buf_ref.at[k], sem).wait()
    z = ALPHA * h_ref[...] + gt[:, 0:1] * buf_ref[0] + gt[:, 1:2] * buf_ref[1]
    o_ref[...] = _layernorm(z, g_ref[...], b_ref[...])


def _combine(h, p0, p1, gates, ys, g, b, tm=1024):
    t, d = h.shape
    tm = min(tm, t)
    return pl.pallas_call(
        functools.partial(_combine_kernel, tm=tm),
        grid=(t // tm,),
        in_specs=[
            pl.BlockSpec((tm,), lambda i: (i,), memory_space=pltpu.SMEM),
            pl.BlockSpec((tm,), lambda i: (i,), memory_space=pltpu.SMEM),
            pl.BlockSpec((SUBLANES, tm), lambda i: (0, i)),
            pl.BlockSpec((tm, d), lambda i: (i, 0)),
            pl.BlockSpec(memory_space=pl.ANY),
            pl.BlockSpec((1, d), lambda i: (0, 0)),
            pl.BlockSpec((1, d), lambda i: (0, 0)),
        ],
        out_specs=pl.BlockSpec((tm, d), lambda i: (i, 0)),
        out_shape=jax.ShapeDtypeStruct((t, d), F32),
        scratch_shapes=[pltpu.VMEM((2, tm, d), F32), pltpu.SemaphoreType.DMA],
        compiler_params=_cparams(("arbitrary",)),
        name="combine_ln",
    )(p0, p1, gates, h, ys, g, b)


def _moe_ln(h, w_pad, b_col, wg, wu, wd, layer, g, b):
    t, d = h.shape
    cap = 2 * t
    nt = (2 * t) // TM_F + N_EXPERTS
    pos, gates, tot = _router(h, w_pad, b_col, cap)
    totals = tot[:, 0]
    ntile = (totals + TM_F - 1) // TM_F
    ends = jnp.cumsum(ntile)
    base = (ends - ntile) * TM_F
    tile_exp = jnp.minimum(jnp.sum((jnp.arange(nt, dtype=jnp.int32)[:, None] >= ends[None, :]).astype(jnp.int32),
                                   axis=1), N_EXPERTS - 1)
    fill_lo = jnp.concatenate([base + totals, ends[-1:] * TM_F]).astype(jnp.int32)
    fill_hi = jnp.concatenate([ends * TM_F, jnp.full((1,), nt * TM_F, jnp.int32)]).astype(jnp.int32)
    pos = _rebase(pos, base.astype(jnp.int32), cap)
    p0, p1 = pos[0], pos[1]
    xs = _dispatch(h, p0, p1, fill_lo, fill_hi, nt * TM_F)
    ys = _ffn(xs, tile_exp.astype(jnp.int32), wg, wu, wd, layer)
    return _combine(h, p0, p1, gates, ys, g, b)


def _retention_tables(c):
    log_g = np.log(1.0 - 2.0 ** (-5.0 - np.arange(N_HEADS, dtype=np.float32))).astype(np.float32)
    idx = np.arange(c, dtype=np.float32)
    diff = idx[:, None] - idx[None, :]
    dmask = np.where(diff >= 0, np.exp(log_g[:, None, None] * np.maximum(diff, 0.0)), 0.0)
    qdec = np.exp(log_g[:, None] * (idx + 1.0))[..., None] * np.ones((1, 1, DK), np.float32)
    kdec = np.exp(log_g[:, None] * (c - 1.0 - idx))[..., None] * np.ones((1, 1, DK), np.float32)
    cdec = np.exp(log_g * c)[:, None, None] * np.ones((1, 1, DV), np.float32)
    return tuple(jnp.asarray(a, F32) for a in (dmask, qdec, kdec, cdec))


def _tri(c, block=None):
    block = c if block is None else block
    return jnp.asarray(np.kron(np.eye(c // block, dtype=np.float32), np.tril(np.ones((block, block), np.float32))),
                       BF16)


def _pad_cols(w, n):
    return jnp.pad(w, ((0, 0), (0, n - w.shape[1])))


def kernel(x, positions, w_in_ab, ret_norm_g, ml_conv_w, ml_conv_b, ml_b_i, ml_b_f, ml_norm_g, w_out_ab,
           w_in_cd, gla_w_alpha, gla_b_alpha, gla_norm_g, ssd_conv_w, ssd_conv_b, ssd_dt_bias, ssd_a_log, ssd_d,
           ssd_norm_g, w_out_cd, w_router, b_router, moe_w_gate, moe_w_up, moe_w_down,
           ln_mix_g, ln_mix_b, ln_ffn_g, ln_ffn_b):
    b, s, d = x.shape
    t = b * s
    c = min(256, s)
    cg = min(64, c)
    h = x.reshape(t, d)
    row = lambda v: v.reshape(1, -1).astype(F32)

    w_pad = _pad_cols(w_router.astype(F32), LANES)
    b_col = b_router.astype(F32).reshape(N_EXPERTS, 1)

    def ffn(hh, layer):
        return _moe_ln(hh, w_pad, b_col, moe_w_gate.astype(F32), moe_w_up.astype(F32), moe_w_down.astype(F32), layer,
                       row(ln_ffn_g[layer]), row(ln_ffn_b[layer]))

    half = np.concatenate([np.arange(0, DK, 2), np.arange(1, DK, 2)])
    perm = np.concatenate([hd * DK + half for hd in range(N_HEADS)])
    w = w_in_ab[0]
    nqk = N_HEADS * DK
    w_main = jnp.concatenate([w[:, :nqk][:, perm], w[:, nqk:2 * nqk][:, perm], w[:, 2 * nqk:P_MAIN]], axis=1)
    inv = ROPE_BASE ** (-jnp.arange(0, DK, 2, dtype=F32) / DK)
    cc, ss = _rope_tables(positions.reshape(t, 1), jnp.concatenate([inv, inv]).reshape(1, DK))
    gate_b = _pad_cols(jnp.concatenate([ml_b_i[0], ml_b_f[0]]).reshape(1, -1).astype(F32), LANES)
    h = _layer_ab(h, w_main.astype(BF16), _pad_cols(w[:, P_MAIN:], LANES).astype(BF16), cc, ss,
                  _retention_tables(c) + (_tri(c),), row(ret_norm_g[0]), row(ml_norm_g[0]),
                  ml_conv_w[0].astype(F32), row(ml_conv_b[0]), gate_b,
                  w_out_ab[0].astype(BF16), row(ln_mix_g[0]), row(ln_mix_b[0]), b, s, c)
    h = ffn(h, 0)

    w = w_in_cd[0]
    g0 = 2 * nqk + 2 * N_HEADS * DV
    s0 = g0 + GLA_RANK
    s1 = s0 + D_MODEL + D_MODEL + 2 * SSD_G * SSD_N
    w_main = jnp.concatenate([w[:, :g0], w[:, s0:s1]], axis=1)
    w_small = _pad_cols(jnp.concatenate([w[:, g0:s0], w[:, s1:]], axis=1), LANES)
    wal = jnp.pad(gla_w_alpha[0], ((0, LANES - GLA_RANK), (0, 0))).astype(BF16)
    lane_pad = lambda v: jnp.pad(v.reshape(1, -1).astype(F32), ((0, 0), (GLA_RANK, LANES - GLA_RANK - SSD_H)))
    dskip = jnp.repeat(ssd_d[0].astype(F32), SSD_P).reshape(1, -1)
    h = _layer_cd(h, w_main.astype(BF16), w_small.astype(BF16), _tri(c), _tri(c, cg), wal, row(gla_b_alpha[0]),
                  row(gla_norm_g[0]), ssd_conv_w[0].astype(F32), row(ssd_conv_b[0]), lane_pad(ssd_dt_bias[0]),
                  lane_pad(ssd_a_log[0]), dskip, row(ssd_norm_g[0]),
                  w_out_cd[0].astype(BF16), row(ln_mix_g[1]), row(ln_mix_b[1]), b, s, c, cg)
    h = ffn(h, 1)
    return h.reshape(b, s, d)
```

```python
import functools
import math

import jax
import jax.numpy as jnp
import numpy as np
from jax import lax
from jax.experimental import pallas as pl
from jax.experimental.pallas import tpu as pltpu

F32 = jnp.float32
BF16 = jnp.bfloat16

D_MODEL = 1024
DEPTH = 2
LN_EPS = 1e-5
NORM_EPS = 1e-6
N_HEADS = 4
DK = D_MODEL // 8
DV = D_MODEL // 4
ROPE_BASE = 10000.0
CONV_K = 4
GLA_RANK = 16
GLA_TAU = 16.0
SSD_P = 64
SSD_H = D_MODEL // SSD_P
SSD_G = 4
SSD_HPG = SSD_H // SSD_G
SSD_N = 128
N_EXPERTS = 16
N_GROUPS = 4
EPG = N_EXPERTS // N_GROUPS
D_FF = D_MODEL // 2
ALPHA = (2.0 * DEPTH) ** 0.25

LANES = 128
SUBLANES = 8
VMEM_LIMIT = 56 * 1024 * 1024

P_MAIN = 6 * D_MODEL
TM_F = 512
SSD_GB = 1
PROJ_PIECE = 256
PROJ_SITES = 7
NEG_BIG = -1e30


def _cparams(sem):
    return pltpu.CompilerParams(dimension_semantics=sem, vmem_limit_bytes=VMEM_LIMIT)


def _dot(a, b):
    return jnp.dot(a, b, preferred_element_type=F32)


def _dot_nt(a, b):
    return lax.dot_general(a, b, (((1,), (1,)), ((), ())), preferred_element_type=F32)


def _dot_tn(a, b):
    return lax.dot_general(a, b, (((0,), (0,)), ((), ())), preferred_element_type=F32)


def _sigmoid(x):
    return 1.0 / (1.0 + jnp.exp(-x))


def _silu(x):
    return x * _sigmoid(x)


def _log_sigmoid(x):
    return jnp.minimum(x, 0.0) - jnp.log(1.0 + jnp.exp(-jnp.abs(x)))


def _softplus(x):
    return jnp.maximum(x, 0.0) + jnp.log(1.0 + jnp.exp(-jnp.abs(x)))


def _tri_cumsum(tri, x):
    hi = x.astype(BF16)
    r1 = x - hi.astype(F32)
    mid = r1.astype(BF16)
    lo = (r1 - mid.astype(F32)).astype(BF16)
    return _dot(tri, hi) + _dot(tri, mid) + _dot(tri, lo)


def _head_norm(y, g, center):
    if center:
        y = y - jnp.mean(y, axis=-1, keepdims=True)
    return y * lax.rsqrt(jnp.mean(y * y, axis=-1, keepdims=True) + NORM_EPS) * g


def _in_projection(x_ref, wm_ref, ws_ref, p_ref, s_ref):
    xb = x_ref[...].astype(BF16)
    for j in range(P_MAIN // D_MODEL):
        sl = slice(j * D_MODEL, (j + 1) * D_MODEL)
        p_ref[:, sl] = _dot(xb, wm_ref[:, sl]).astype(BF16)
    s_ref[...] = _dot(xb, ws_ref[...])


def _pipelined_in_projection(x_ref, xn_ref, wm_ref, ws_ref, pbuf_ref, sbuf_ref, xb_ref):
    step = pl.program_id(0)
    slot = step % 2

    @pl.when(step == 0)
    def _():
        _in_projection(x_ref, wm_ref, ws_ref, pbuf_ref.at[0], sbuf_ref.at[0])

    nxt_p = pbuf_ref.at[1 - slot]
    nxt_s = sbuf_ref.at[1 - slot]
    xb_ref[...] = xn_ref[...].astype(BF16)

    def main_piece(j):
        def run():
            sl = slice(j * PROJ_PIECE, (j + 1) * PROJ_PIECE)
            nxt_p[:, sl] = _dot(xb_ref[...], wm_ref[:, sl]).astype(BF16)
        return run

    def small_piece():
        nxt_s[...] = _dot(xb_ref[...], ws_ref[...])

    thunks = [main_piece(j) for j in range(P_MAIN // PROJ_PIECE)] + [small_piece]

    def advance(site):
        for k in range(site * len(thunks) // PROJ_SITES, (site + 1) * len(thunks) // PROJ_SITES):
            thunks[k]()

    return pbuf_ref.at[slot], sbuf_ref[slot], advance


def _layernorm(z, g, b):
    zc = z - jnp.mean(z, axis=-1, keepdims=True)
    return zc * lax.rsqrt(jnp.mean(zc * zc, axis=-1, keepdims=True) + LN_EPS) * g + b


def _rope_kernel(pos_ref, inv_ref, cc_ref, ss_ref):
    half = pos_ref.shape[0] // 2
    lane = lax.broadcasted_iota(jnp.int32, (half, DK), 1)
    low = lane < DK // 2
    ang = jnp.where(low, pos_ref[0:half, :], pos_ref[half:2 * half, :]).astype(F32) * inv_ref[...]
    c = jnp.cos(ang)
    s = jnp.sin(ang)
    c_sw = pltpu.roll(c, DK // 2, 1)
    s_sw = pltpu.roll(s, DK // 2, 1)
    cc_ref[0:half, :] = jnp.where(low, c, c_sw)
    cc_ref[half:2 * half, :] = jnp.where(low, c_sw, c)
    ss_ref[0:half, :] = jnp.where(low, -s, s_sw)
    ss_ref[half:2 * half, :] = jnp.where(low, -s_sw, s)


def _rope_tables(pos_col, inv2, tm=2048):
    t = pos_col.shape[0]
    tm = min(tm, t)
    return pl.pallas_call(
        _rope_kernel,
        grid=(t // tm,),
        in_specs=[pl.BlockSpec((tm, 1), lambda i: (i, 0)), pl.BlockSpec((1, DK), lambda i: (0, 0))],
        out_specs=[pl.BlockSpec((tm, DK), lambda i: (i, 0))] * 2,
        out_shape=[jax.ShapeDtypeStruct((t, DK), F32)] * 2,
        compiler_params=_cparams(("parallel",)),
        name="rope_tables",
    )(pos_col, inv2)


def _causal_conv_tile(hist_ref, u, w_ref, b_ref, c):
    hist_ref[pl.ds(SUBLANES, c), :] = u
    acc = b_ref[...] + w_ref[CONV_K - 1:CONV_K, :] * u
    for j in range(CONV_K - 1):
        acc = acc + w_ref[j:j + 1, :] * hist_ref[pl.ds(SUBLANES - (CONV_K - 1) + j, c), :]
    hist_ref[pl.ds(0, SUBLANES), :] = hist_ref[pl.ds(c, SUBLANES), :]
    return acc


def _layer_ab_kernel(x_ref, xn_ref, wm_ref, ws_ref, cc_ref, ss_ref, dmask_ref, qdec_ref, kdec_ref, cdec_ref,
                     tri_ref, rng_ref, mng_ref, cw_ref, cb_ref, gb_ref, wo_ref, lg_ref, lb_ref, o_ref,
                     pbuf_ref, sbuf_ref, xb_ref, z_ref, rs_ref, mc_ref, mn_ref, mm_ref, hist_ref, *, c, nt):
    @pl.when(pl.program_id(0) % nt == 0)
    def _():
        rs_ref[...] = jnp.zeros_like(rs_ref)
        mc_ref[...] = jnp.zeros_like(mc_ref)
        mn_ref[...] = jnp.zeros_like(mn_ref)
        mm_ref[...] = jnp.zeros_like(mm_ref)
        hist_ref[pl.ds(0, SUBLANES), :] = jnp.zeros((SUBLANES, hist_ref.shape[1]), F32)

    p_ref, small, next_proj = _pipelined_in_projection(x_ref, xn_ref, wm_ref, ws_ref, pbuf_ref, sbuf_ref, xb_ref)
    z_ref[...] = ALPHA * x_ref[...]
    cc = cc_ref[...]
    ss = ss_ref[...]
    kscale = DK ** -0.5

    def rot(t):
        return t * cc + pltpu.roll(t, DK // 2, 1) * ss

    heads = range(N_HEADS)
    qs = [rot(p_ref[:, h * DK:(h + 1) * DK].astype(F32)) for h in heads]
    ks = [rot(p_ref[:, 512 + h * DK:512 + (h + 1) * DK].astype(F32)) * kscale for h in heads]
    vs = [p_ref[:, 1024 + h * DV:1024 + (h + 1) * DV] for h in heads]
    next_proj(0)
    scs = [_dot_nt(qs[h].astype(BF16), ks[h].astype(BF16)) * dmask_ref[h] for h in heads]
    next_proj(1)
    ys = [_dot(scs[h].astype(BF16), vs[h]) + _dot((qs[h] * qdec_ref[h]).astype(BF16), rs_ref[h].astype(BF16))
          for h in heads]
    next_proj(2)
    for h in heads:
        rs_ref[h] = cdec_ref[h] * rs_ref[h] + _dot_tn((ks[h] * kdec_ref[h]).astype(BF16), vs[h])
    yns = [_head_norm(ys[h], rng_ref[:, h * DV:(h + 1) * DV], True)
           * _silu(p_ref[:, 2048 + h * DV:2048 + (h + 1) * DV].astype(F32)) for h in heads]
    next_proj(3)
    for h in heads:
        z_ref[...] += _dot(yns[h].astype(BF16), wo_ref[h * DV:(h + 1) * DV, :])

    u = p_ref[:, 3072:4096].astype(F32)
    qk = _silu(_causal_conv_tile(hist_ref, u, cw_ref, cb_ref, c))
    gates = small + gb_ref[...]
    lane = lax.broadcasted_iota(jnp.int32, gates.shape, 1)
    is_f = (lane // N_HEADS) == 1
    logf = jnp.where(is_f, _log_sigmoid(gates), 0.0)
    bcum = _tri_cumsum(tri_ref[...], logf)
    pm = jnp.where(lane < N_HEADS, gates, bcum)
    pmt = pm.T
    row = lax.broadcasted_iota(jnp.int32, (c, c), 0)
    col = lax.broadcasted_iota(jnp.int32, (c, c), 1)
    causal = row >= col
    qs = [qk[:, h * DK:(h + 1) * DK] for h in heads]
    ks = [qk[:, 512 + h * DK:512 + (h + 1) * DK] * kscale for h in heads]
    vs = [p_ref[:, 4096 + h * DV:4096 + (h + 1) * DV] for h in heads]
    i_cols = [pm[:, h:h + 1] for h in heads]
    b_cols = [pm[:, N_HEADS + h:N_HEADS + h + 1] for h in heads]
    m_prevs = [mm_ref[h][:, 0:1] for h in heads]
    a_s = [b_cols[h] + m_prevs[h] for h in heads]
    ds = [jnp.where(causal, b_cols[h] - pmt[N_HEADS + h:N_HEADS + h + 1, :] + pmt[h:h + 1, :], NEG_BIG)
          for h in heads]
    m_ts = [jnp.maximum(a_s[h], jnp.max(ds[h], axis=1, keepdims=True)) for h in heads]
    w_inters = [jnp.exp(a_s[h] - m_ts[h]) for h in heads]
    next_proj(4)
    s_qks = [_dot_nt(qs[h].astype(BF16), ks[h].astype(BF16)) * jnp.exp(ds[h] - m_ts[h]) for h in heads]
    next_proj(5)
    nums = [_dot(s_qks[h].astype(BF16), vs[h]) + w_inters[h] * _dot(qs[h].astype(BF16), mc_ref[h].astype(BF16))
            for h in heads]
    dens = [jnp.sum(s_qks[h], axis=1, keepdims=True)
            + w_inters[h] * jnp.sum(qs[h] * mn_ref[h], axis=1, keepdims=True) for h in heads]
    hhs = [nums[h] / jnp.maximum(jnp.abs(dens[h]), jnp.exp(-m_ts[h])) for h in heads]
    next_proj(6)
    for h in heads:
        b_tot = b_cols[h][c - 1:c, :]
        lw = b_tot - b_cols[h] + i_cols[h]
        m_new = jnp.maximum(b_tot + m_prevs[h], jnp.max(lw, axis=0, keepdims=True))
        kd = ks[h] * jnp.exp(lw - m_new)
        decay = jnp.exp(b_tot + m_prevs[h] - m_new)
        mc_ref[h] = decay * mc_ref[h] + _dot_tn(kd.astype(BF16), vs[h])
        mn_ref[h] = decay * mn_ref[h] + jnp.sum(kd, axis=0, keepdims=True)
        mm_ref[h] = jnp.broadcast_to(m_new, (1, LANES))
    yns = [_head_norm(hhs[h] * _sigmoid(p_ref[:, 5120 + h * DV:5120 + (h + 1) * DV].astype(F32)),
                      mng_ref[:, h * DV:(h + 1) * DV], True) for h in heads]
    for h in heads:
        z_ref[...] += _dot(yns[h].astype(BF16), wo_ref[D_MODEL + h * DV:D_MODEL + (h + 1) * DV, :])

    o_ref[...] = _layernorm(z_ref[...], lg_ref[...], lb_ref[...])


def _layer_ab(x, w_main, w_small, cc, ss, tabs, ret_g, ml_g, conv_w, conv_b, gate_b, w_out, ln_g, ln_b, b, s, c):
    nt = s // c
    t, d = x.shape
    n = b * nt
    tok = lambda i: (i, 0)
    nxt = lambda i: (jnp.minimum(i + 1, n - 1), 0)
    full2 = lambda i: (0, 0)
    full3 = lambda i: (0, 0, 0)
    dmask, qdec, kdec, cdec, tri = tabs
    return pl.pallas_call(
        functools.partial(_layer_ab_kernel, c=c, nt=nt),
        grid=(n,),
        in_specs=[
            pl.BlockSpec((c, d), tok),
            pl.BlockSpec((c, d), nxt),
            pl.BlockSpec((d, P_MAIN), full2),
            pl.BlockSpec((d, LANES), full2),
            pl.BlockSpec((c, DK), tok),
            pl.BlockSpec((c, DK), tok),
            pl.BlockSpec((N_HEADS, c, c), full3),
            pl.BlockSpec((N_HEADS, c, DK), full3),
            pl.BlockSpec((N_HEADS, c, DK), full3),
            pl.BlockSpec((N_HEADS, 1, DV), full3),
            pl.BlockSpec((c, c), full2),
            pl.BlockSpec((1, D_MODEL), full2),
            pl.BlockSpec((1, D_MODEL), full2),
            pl.BlockSpec((CONV_K, D_MODEL), full2),
            pl.BlockSpec((1, D_MODEL), full2),
            pl.BlockSpec((1, LANES), full2),
            pl.BlockSpec((2 * D_MODEL, d), full2),
            pl.BlockSpec((1, d), full2),
            pl.BlockSpec((1, d), full2),
        ],
        out_specs=pl.BlockSpec((c, d), tok),
        out_shape=jax.ShapeDtypeStruct((t, d), F32),
        scratch_shapes=[
            pltpu.VMEM((2, c, P_MAIN), BF16),
            pltpu.VMEM((2, c, LANES), F32),
            pltpu.VMEM((c, d), BF16),
            pltpu.VMEM((c, d), F32),
            pltpu.VMEM((N_HEADS, DK, DV), F32),
            pltpu.VMEM((N_HEADS, DK, DV), F32),
            pltpu.VMEM((N_HEADS, 1, DK), F32),
            pltpu.VMEM((N_HEADS, 1, LANES), F32),
            pltpu.VMEM((SUBLANES + c, D_MODEL), F32),
        ],
        compiler_params=_cparams(("arbitrary",)),
        name="layer_ab",
    )(x, x, w_main, w_small, cc, ss, dmask, qdec, kdec, cdec, tri, ret_g, ml_g, conv_w, conv_b, gate_b,
      w_out, ln_g, ln_b)


def _layer_cd_kernel(x_ref, xn_ref, wm_ref, ws_ref, tri_ref, trig_ref, wal_ref, bal_ref, gng_ref, cw_ref, cb_ref,
                     dtb_ref, alog_ref, dskip_ref, sng_ref, wo_ref, lg_ref, lb_ref, o_ref,
                     pbuf_ref, sbuf_ref, xb_ref, z_ref, gs_ref, ss_ref, hist_ref, *, c, cg, nt):
    @pl.when(pl.program_id(0) % nt == 0)
    def _():
        gs_ref[...] = jnp.zeros_like(gs_ref)
        ss_ref[...] = jnp.zeros_like(ss_ref)
        hist_ref[pl.ds(0, SUBLANES), :] = jnp.zeros((SUBLANES, hist_ref.shape[1]), F32)

    p_ref, small, next_proj = _pipelined_in_projection(x_ref, xn_ref, wm_ref, ws_ref, pbuf_ref, sbuf_ref, xb_ref)
    z_ref[...] = ALPHA * x_ref[...]

    log_alpha = _log_sigmoid(_dot(small.astype(BF16), wal_ref[...]) + bal_ref[...]) * (1.0 / GLA_TAU)
    qscale = DK ** -0.5
    trig = trig_ref[...]
    rowg = lax.broadcasted_iota(jnp.int32, (cg, cg), 0)
    colg = lax.broadcasted_iota(jnp.int32, (cg, cg), 1)
    causal_g = rowg >= colg
    heads = range(N_HEADS)
    subs = range(c // cg)
    rows = lambda j: slice(j * cg, (j + 1) * cg)
    cols = lambda h: slice(h * DK, (h + 1) * DK)
    bcum = _tri_cumsum(trig, log_alpha)
    b_tots = [bcum[(j + 1) * cg - 1:(j + 1) * cg, :] for j in subs]
    e_end = jnp.concatenate([jnp.exp(b_tots[j] - bcum[rows(j), :]) for j in subs], axis=0)
    next_proj(0)
    q_all = p_ref[:, 0:N_HEADS * DK].astype(F32) * qscale
    k_all = p_ref[:, N_HEADS * DK:2 * N_HEADS * DK].astype(F32)
    qe = (q_all * jnp.exp(bcum)).astype(BF16)
    ke = (k_all * jnp.exp(-bcum)).astype(BF16)
    kd = (k_all * e_end).astype(BF16)
    vs = [[p_ref[rows(j), 1024 + h * DV:1024 + (h + 1) * DV] for h in heads] for j in subs]
    next_proj(1)
    scs = [[jnp.where(causal_g, _dot_nt(qe[rows(j), cols(h)], ke[rows(j), cols(h)]), 0.0).astype(BF16)
            for h in heads] for j in subs]
    next_proj(2)
    y_in = [[_dot(scs[j][h], vs[j][h]) for h in heads] for j in subs]
    next_proj(3)
    sts = [gs_ref[h] for h in heads]
    ys = [[] for _ in heads]
    for j in subs:
        for h in heads:
            ys[h].append(y_in[j][h] + _dot_nt(qe[rows(j), cols(h)], sts[h].astype(BF16)))
        e_tot = jnp.exp(b_tots[j])
        sts = [e_tot[:, cols(h)] * sts[h] + _dot_tn(vs[j][h], kd[rows(j), cols(h)]) for h in heads]
    for h in heads:
        gs_ref[h] = sts[h]
    yns = [_head_norm(jnp.concatenate(ys[h], axis=0), gng_ref[:, h * DV:(h + 1) * DV], False)
           * _silu(p_ref[:, 2048 + h * DV:2048 + (h + 1) * DV].astype(F32)) for h in heads]
    z_ref[...] += _dot(jnp.concatenate([yns[h].astype(BF16) for h in heads], axis=1), wo_ref[0:D_MODEL, :])

    u = p_ref[:, 4096:6144].astype(F32)
    xbc = _silu(_causal_conv_tile(hist_ref, u, cw_ref, cb_ref, c))
    lane = lax.broadcasted_iota(jnp.int32, small.shape, 1)
    is_dt = (lane // GLA_RANK) == 1
    dt = jnp.where(is_dt, _softplus(small + dtb_ref[...]), 0.0)
    la = dt * -jnp.exp(alog_ref[...])
    cum = _tri_cumsum(tri_ref[...], la)
    pm = jnp.where(lane < GLA_RANK, _dt_shift(dt), cum)
    pmt = pm.T
    row = lax.broadcasted_iota(jnp.int32, (c, c), 0)
    col = lax.broadcasted_iota(jnp.int32, (c, c), 1)
    causal = row >= col
    for g0 in range(0, SSD_G, SSD_GB):
        groups = range(g0, g0 + SSD_GB)
        hd_all = range(g0 * SSD_HPG, (g0 + SSD_GB) * SSD_HPG)
        if N_HEADS + g0 < PROJ_SITES:
            next_proj(N_HEADS + g0)
        bms = {g: xbc[:, 1024 + g * SSD_N:1024 + (g + 1) * SSD_N].astype(BF16) for g in groups}
        cms = {g: xbc[:, 1536 + g * SSD_N:1536 + (g + 1) * SSD_N].astype(BF16) for g in groups}
        cbs = {g: _dot_nt(cms[g], bms[g]) for g in groups}
        y_inters = {g: _dot(cms[g], ss_ref[g].astype(BF16)) for g in groups}
        x_hs = {hd: xbc[:, hd * SSD_P:(hd + 1) * SSD_P] for hd in hd_all}
        dt_cols = {hd: pm[:, hd:hd + 1] for hd in hd_all}
        cum_cols = {hd: pm[:, GLA_RANK + hd:GLA_RANK + hd + 1] for hd in hd_all}
        ws = {hd: (cbs[hd // SSD_HPG] * jnp.exp(jnp.where(
            causal, cum_cols[hd] - pmt[GLA_RANK + hd:GLA_RANK + hd + 1, :], NEG_BIG))).astype(BF16)
            for hd in hd_all}
        if N_HEADS + g0 + 1 < PROJ_SITES and SSD_GB > 1:
            next_proj(N_HEADS + g0 + 1)
        y_hs = {hd: _dot(ws[hd], (x_hs[hd] * dt_cols[hd]).astype(BF16)) for hd in hd_all}
        y_parts = {hd: y_hs[hd] + jnp.exp(cum_cols[hd]) * y_inters[hd // SSD_HPG][
            :, (hd % SSD_HPG) * SSD_P:(hd % SSD_HPG + 1) * SSD_P]
            + dskip_ref[:, hd * SSD_P:(hd + 1) * SSD_P] * x_hs[hd] for hd in hd_all}
        lasts = {hd: cum_cols[hd][c - 1:c, :] for hd in hd_all}
        xs_parts = {hd: x_hs[hd] * (dt_cols[hd] * jnp.exp(lasts[hd] - cum_cols[hd])) for hd in hd_all}
        dec_parts = {hd: jnp.broadcast_to(jnp.exp(lasts[hd]), (1, SSD_P)) for hd in hd_all}
        for g in groups:
            hds = range(g * SSD_HPG, (g + 1) * SSD_HPG)
            xs = jnp.concatenate([xs_parts[hd] for hd in hds], axis=1).astype(BF16)
            dec = jnp.concatenate([dec_parts[hd] for hd in hds], axis=1)
            ss_ref[g] = dec * ss_ref[g] + _dot_tn(bms[g], xs)
        yns = {g: _head_norm(jnp.concatenate([y_parts[hd] for hd in range(g * SSD_HPG, (g + 1) * SSD_HPG)], axis=1)
                             * _silu(p_ref[:, 3072 + g * DV:3072 + (g + 1) * DV].astype(F32)),
                             sng_ref[:, g * DV:(g + 1) * DV], False) for g in groups}
        for g in groups:
            z_ref[...] += _dot(yns[g].astype(BF16), wo_ref[D_MODEL + g * DV:D_MODEL + (g + 1) * DV, :])

    o_ref[...] = _layernorm(z_ref[...], lg_ref[...], lb_ref[...])


def _dt_shift(dt):
    return pltpu.roll(dt, LANES - GLA_RANK, 1)


def _layer_cd(x, w_main, w_small, tri, trig, wal, bal, gla_g, conv_w, conv_b, dtb, alog, dskip, ssd_g,
              w_out, ln_g, ln_b, b, s, c, cg):
    nt = s // c
    t, d = x.shape
    n = b * nt
    tok = lambda i: (i, 0)
    nxt = lambda i: (jnp.minimum(i + 1, n - 1), 0)
    full2 = lambda i: (0, 0)
    ch = D_MODEL + 2 * SSD_G * SSD_N
    return pl.pallas_call(
        functools.partial(_layer_cd_kernel, c=c, cg=cg, nt=nt),
        grid=(n,),
        in_specs=[
            pl.BlockSpec((c, d), tok),
            pl.BlockSpec((c, d), nxt),
            pl.BlockSpec((d, P_MAIN), full2),
            pl.BlockSpec((d, LANES), full2),
            pl.BlockSpec((c, c), full2),
            pl.BlockSpec((c, c), full2),
            pl.BlockSpec((LANES, N_HEADS * DK), full2),
            pl.BlockSpec((1, N_HEADS * DK), full2),
            pl.BlockSpec((1, D_MODEL), full2),
            pl.BlockSpec((CONV_K, ch), full2),
            pl.BlockSpec((1, ch), full2),
            pl.BlockSpec((1, LANES), full2),
            pl.BlockSpec((1, LANES), full2),
            pl.BlockSpec((1, D_MODEL), full2),
            pl.BlockSpec((1, D_MODEL), full2),
            pl.BlockSpec((2 * D_MODEL, d), full2),
            pl.BlockSpec((1, d), full2),
            pl.BlockSpec((1, d), full2),
        ],
        out_specs=pl.BlockSpec((c, d), tok),
        out_shape=jax.ShapeDtypeStruct((t, d), F32),
        scratch_shapes=[
            pltpu.VMEM((2, c, P_MAIN), BF16),
            pltpu.VMEM((2, c, LANES), F32),
            pltpu.VMEM((c, d), BF16),
            pltpu.VMEM((c, d), F32),
            pltpu.VMEM((N_HEADS, DV, DK), F32),
            pltpu.VMEM((SSD_G, SSD_N, SSD_HPG * SSD_P), F32),
            pltpu.VMEM((SUBLANES + c, ch), F32),
        ],
        compiler_params=_cparams(("arbitrary",)),
        name="layer_cd",
    )(x, x, w_main, w_small, tri, trig, wal, bal, gla_g, conv_w, conv_b, dtb, alog, dskip, ssd_g, w_out, ln_g, ln_b)


def _split2(x):
    hi = x.astype(BF16)
    return hi, (x - hi.astype(F32)).astype(BF16)


def _router_kernel(h_ref, w_ref, b_ref, ltri_ref, utri_ref, pos_ref, gate_ref, tot_ref, carry_ref, *, cap):
    @pl.when(pl.program_id(0) == 0)
    def _():
        carry_ref[...] = jnp.zeros_like(carry_ref)

    h_hi, h_lo = _split2(h_ref[...])
    w_hi, w_lo = _split2(w_ref[...])
    logits = (_dot(h_hi, w_hi) + _dot(h_lo, w_hi) + _dot(h_hi, w_lo)).T[:N_EXPERTS, :]
    e = jnp.exp(logits - jnp.max(logits, axis=0, keepdims=True))
    probs = e / jnp.sum(e, axis=0, keepdims=True)
    sel = probs + b_ref[...]
    row = lax.broadcasted_iota(jnp.int32, sel.shape, 0)
    pos = row & (EPG - 1)
    gidx = row // EPG

    def member(a, k):
        fwd = pltpu.roll(a, N_EXPERTS - k, 0)
        back = pltpu.roll(a, EPG - k, 0)
        return jnp.where(pos + k < EPG, fwd, back)

    others = [member(sel, k) for k in range(1, EPG)]
    vals = [sel] + others
    top2 = None
    for i in range(EPG):
        for j in range(i + 1, EPG):
            pair = vals[i] + vals[j]
            top2 = pair if top2 is None else jnp.maximum(top2, pair)
    best = jnp.ones(sel.shape, jnp.int32)
    for k in range(1, N_GROUPS):
        other = pltpu.roll(top2, EPG * k, 0)
        wins = jnp.where(gidx >= k, jnp.where(top2 > other, 1, 0), jnp.where(top2 >= other, 1, 0))
        best = best * wins
    rank = jnp.zeros(sel.shape, jnp.int32)
    for k in range(1, EPG):
        v = others[k - 1]
        tie = jnp.where(pos + k >= EPG, 1, 0)
        rank = rank + jnp.where(v > sel, 1, jnp.where(v == sel, tie, 0))
    chosen = jnp.where(best * jnp.where(rank < 2, 1, 0) > 0, 1.0, 0.0)
    gsel = chosen * probs
    comb = gsel / jnp.sum(gsel, axis=0, keepdims=True)

    chosen_b = chosen.astype(BF16)
    lower = _dot(ltri_ref[...], chosen_b)
    first = chosen * jnp.where(lower == 0.0, 1.0, 0.0)
    second = chosen - first
    run = carry_ref[:, 0:1]
    parts = []
    for blk in range(sel.shape[1] // LANES):
        sl = slice(blk * LANES, (blk + 1) * LANES)
        parts.append(run + _dot(chosen_b[:, sl], utri_ref[...]))
        run = run + jnp.sum(chosen[:, sl], axis=1, keepdims=True)
    slot = jnp.concatenate(parts, axis=1) + row.astype(F32) * float(cap)
    pos0 = jnp.sum(first * slot, axis=0, keepdims=True)
    pos1 = jnp.sum(second * slot, axis=0, keepdims=True)
    g0 = jnp.sum(first * comb, axis=0, keepdims=True)
    g1 = jnp.sum(second * comb, axis=0, keepdims=True)
    r8 = lax.broadcasted_iota(jnp.int32, (SUBLANES, sel.shape[1]), 0)
    pos_ref[...] = jnp.where(r8 == 0, pos0, jnp.where(r8 == 1, pos1, 0.0)).astype(jnp.int32)
    gate_ref[...] = jnp.where(r8 == 0, g0, jnp.where(r8 == 1, g1, 0.0))
    total = jnp.broadcast_to(run, carry_ref.shape)
    carry_ref[...] = total
    tot_ref[...] = total.astype(jnp.int32)


def _router(h, w_pad, b_col, cap, tm=1024):
    t, d = h.shape
    tm = min(tm, t)
    ltri = jnp.asarray(np.tril(np.ones((N_EXPERTS, N_EXPERTS), np.float32), -1), BF16)
    utri = jnp.asarray(np.triu(np.ones((LANES, LANES), np.float32), 1), BF16)
    return pl.pallas_call(
        functools.partial(_router_kernel, cap=cap),
        grid=(t // tm,),
        in_specs=[
            pl.BlockSpec((tm, d), lambda i: (i, 0)),
            pl.BlockSpec((d, LANES), lambda i: (0, 0)),
            pl.BlockSpec((N_EXPERTS, 1), lambda i: (0, 0)),
            pl.BlockSpec((N_EXPERTS, N_EXPERTS), lambda i: (0, 0)),
            pl.BlockSpec((LANES, LANES), lambda i: (0, 0)),
        ],
        out_specs=[
            pl.BlockSpec((SUBLANES, tm), lambda i: (0, i)),
            pl.BlockSpec((SUBLANES, tm), lambda i: (0, i)),
            pl.BlockSpec((N_EXPERTS, LANES), lambda i: (0, 0)),
        ],
        out_shape=[
            jax.ShapeDtypeStruct((SUBLANES, t), jnp.int32),
            jax.ShapeDtypeStruct((SUBLANES, t), F32),
            jax.ShapeDtypeStruct((N_EXPERTS, LANES), jnp.int32),
        ],
        scratch_shapes=[pltpu.VMEM((N_EXPERTS, LANES), F32)],
        compiler_params=_cparams(("arbitrary",)),
        name="router",
    )(h, w_pad, b_col, ltri, utri)


def _rebase_kernel(base_ref, pos_ref, o_ref, *, cap):
    p = pos_ref[...]
    e = p // cap
    out = p - e * cap
    for k in range(N_EXPERTS):
        out = out + jnp.where(e == k, base_ref[k], 0)
    o_ref[...] = out


def _rebase(pos, base, cap, tm=8192):
    t = pos.shape[1]
    tm = min(tm, t)
    return pl.pallas_call(
        functools.partial(_rebase_kernel, cap=cap),
        grid_spec=pltpu.PrefetchScalarGridSpec(
            num_scalar_prefetch=1,
            grid=(t // tm,),
            in_specs=[pl.BlockSpec((SUBLANES, tm), lambda i, base: (0, i))],
            out_specs=pl.BlockSpec((SUBLANES, tm), lambda i, base: (0, i)),
        ),
        out_shape=jax.ShapeDtypeStruct(pos.shape, jnp.int32),
        compiler_params=_cparams(("arbitrary",)),
        name="rebase",
    )(base, pos)


def _dispatch_kernel(lo_ref, hi_ref, p0_ref, p1_ref, h_ref, xs_ref, sem, *, tm):
    def issue(i, carry):
        for u in range(SUBLANES):
            r = i * SUBLANES + u
            src = h_ref.at[pl.ds(r, 1), :]
            pltpu.make_async_copy(src, xs_ref.at[pl.ds(p0_ref[r], 1), :], sem).start(priority=0)
            pltpu.make_async_copy(src, xs_ref.at[pl.ds(p1_ref[r], 1), :], sem).start(priority=1)
        return carry

    lax.fori_loop(0, tm // SUBLANES, issue, 0)

    @pl.when(pl.program_id(0) == pl.num_programs(0) - 1)
    def _():
        src = h_ref.at[pl.ds(0, 1), :]

        def fill(r, carry):
            pltpu.make_async_copy(src, xs_ref.at[pl.ds(r, 1), :], sem).start()
            return carry

        def drain(r, carry):
            pltpu.make_async_copy(src, xs_ref.at[pl.ds(0, 1), :], sem).wait()
            return carry

        for k in range(N_EXPERTS + 1):
            lax.fori_loop(lo_ref[k], hi_ref[k], fill, 0)
            lax.fori_loop(lo_ref[k], hi_ref[k], drain, 0)

    whole = pltpu.make_async_copy(h_ref, xs_ref.at[pl.ds(0, tm), :], sem)
    whole.wait()
    whole.wait()


def _dispatch(h, p0, p1, fill_lo, fill_hi, n_rows, tm=1024):
    t, d = h.shape
    tm = min(tm, t)
    pos_spec = pl.BlockSpec((tm,), lambda i, lo, hi: (i,), memory_space=pltpu.SMEM)
    return pl.pallas_call(
        functools.partial(_dispatch_kernel, tm=tm),
        grid_spec=pltpu.PrefetchScalarGridSpec(
            num_scalar_prefetch=2,
            grid=(t // tm,),
            in_specs=[pos_spec, pos_spec, pl.BlockSpec((tm, d), lambda i, lo, hi: (i, 0))],
            out_specs=pl.BlockSpec(memory_space=pl.ANY),
            scratch_shapes=[pltpu.SemaphoreType.DMA],
        ),
        out_shape=jax.ShapeDtypeStruct((n_rows, d), F32),
        compiler_params=_cparams(("arbitrary",)),
        name="dispatch",
    )(fill_lo, fill_hi, p0, p1, h)


def _ffn_kernel(exp_ref, x_ref, wg_ref, wu_ref, wd_ref, y_ref, wgb_ref, wub_ref, wdb_ref):
    i = pl.program_id(0)

    @pl.when(jnp.logical_or(i == 0, exp_ref[i] != exp_ref[jnp.maximum(i - 1, 0)]))
    def _():
        wgb_ref[...] = wg_ref[...].astype(BF16)
        wub_ref[...] = wu_ref[...].astype(BF16)
        wdb_ref[...] = wd_ref[...].astype(BF16)

    xb = x_ref[...].astype(BF16)
    act = _silu(_dot(xb, wgb_ref[...])) * _dot(xb, wub_ref[...])
    y_ref[...] = _dot(act.astype(BF16), wdb_ref[...])


def _ffn(xs, tile_exp, wg, wu, wd, layer):
    n, d = xs.shape
    return pl.pallas_call(
        _ffn_kernel,
        grid_spec=pltpu.PrefetchScalarGridSpec(
            num_scalar_prefetch=1,
            grid=(n // TM_F,),
            in_specs=[
                pl.BlockSpec((TM_F, d), lambda i, ex: (i, 0)),
                pl.BlockSpec((None, None, d, D_FF), lambda i, ex: (layer, ex[i], 0, 0)),
                pl.BlockSpec((None, None, d, D_FF), lambda i, ex: (layer, ex[i], 0, 0)),
                pl.BlockSpec((None, None, D_FF, d), lambda i, ex: (layer, ex[i], 0, 0)),
            ],
            out_specs=pl.BlockSpec((TM_F, d), lambda i, ex: (i, 0)),
            scratch_shapes=[pltpu.VMEM((d, D_FF), BF16), pltpu.VMEM((d, D_FF), BF16), pltpu.VMEM((D_FF, d), BF16)],
        ),
        out_shape=jax.ShapeDtypeStruct((n, d), F32),
        compiler_params=_cparams(("arbitrary",)),
        name="expert_ffn",
    )(tile_exp, xs, wg, wu, wd)


def _combine_kernel(p0_ref, p1_ref, p0n_ref, p1n_ref, gate_ref, h_ref, ys_ref, g_ref, b_ref, o_ref,
                    buf_ref, sem, *, tm):
    step = pl.program_id(0)
    last = pl.num_programs(0) - 1

    def gather(pa_ref, pb_ref, slot):
        def issue(i, carry):
            for u in range(SUBLANES):
                r = i * SUBLANES + u
                pltpu.make_async_copy(ys_ref.at[pl.ds(pa_ref[r], 1), :], buf_ref.at[slot, 0, pl.ds(r, 1), :],
                                      sem.at[slot]).start(priority=0)
                pltpu.make_async_copy(ys_ref.at[pl.ds(pb_ref[r], 1), :], buf_ref.at[slot, 1, pl.ds(r, 1), :],
                                      sem.at[slot]).start(priority=1)
            return carry

        lax.fori_loop(0, tm // SUBLANES, issue, 0)

    @pl.when(step == 0)
    def _():
        gather(p0_ref, p1_ref, 0)

    for slot in range(2):
        @pl.when(jnp.logical_and(step < last, (step + 1) % 2 == slot))
        def _(slot=slot):
            gather(p0n_ref, p1n_ref, slot)

    for slot in range(2):
        @pl.when(step % 2 == slot)
        def _(slot=slot):
            gt = gate_ref[...].T
            for k in range(2):
                pltpu.make_async_copy(ys_ref.at[pl.ds(0, tm), :], buf_ref.at[slot, k], sem.at[slot]).wait()
            z = ALPHA * h_ref[...] + gt[:, 0:1] * buf_ref[slot, 0] + gt[:, 1:2] * buf_ref[slot, 1]
            o_ref[...] = _layernorm(z, g_ref[...], b_ref[...])


def _combine(h, p0, p1, gates, ys, g, b, tm=1024):
    t, d = h.shape
    tm = min(tm, t)
    n = t // tm
    nxt = pl.BlockSpec((tm,), lambda i: (jnp.minimum(i + 1, n - 1),), memory_space=pltpu.SMEM)
    return pl.pallas_call(
        functools.partial(_combine_kernel, tm=tm),
        grid=(n,),
        in_specs=[
            pl.BlockSpec((tm,), lambda i: (i,), memory_space=pltpu.SMEM),
            pl.BlockSpec((tm,), lambda i: (i,), memory_space=pltpu.SMEM),
            nxt,
            nxt,
            pl.BlockSpec((SUBLANES, tm), lambda i: (0, i)),
            pl.BlockSpec((tm, d), lambda i: (i, 0)),
            pl.BlockSpec(memory_space=pl.ANY),
            pl.BlockSpec((1, d), lambda i: (0, 0)),
            pl.BlockSpec((1, d), lambda i: (0, 0)),
        ],
        out_specs=pl.BlockSpec((tm, d), lambda i: (i, 0)),
        out_shape=jax.ShapeDtypeStruct((t, d), F32),
        scratch_shapes=[pltpu.VMEM((2, 2, tm, d), F32), pltpu.SemaphoreType.DMA((2,))],
        compiler_params=_cparams(("arbitrary",)),
        name="combine_ln",
    )(p0, p1, p0, p1, gates, h, ys, g, b)


def _moe_ln(h, w_pad, b_col, wg, wu, wd, layer, g, b):
    t, d = h.shape
    cap = 2 * t
    nt = (2 * t) // TM_F + N_EXPERTS
    pos, gates, tot = _router(h, w_pad, b_col, cap)
    totals = tot[:, 0]
    ntile = (totals + TM_F - 1) // TM_F
    ends = jnp.cumsum(ntile)
    base = (ends - ntile) * TM_F
    tile_exp = jnp.minimum(jnp.sum((jnp.arange(nt, dtype=jnp.int32)[:, None] >= ends[None, :]).astype(jnp.int32),
                                   axis=1), N_EXPERTS - 1)
    fill_lo = jnp.concatenate([base + totals, ends[-1:] * TM_F]).astype(jnp.int32)
    fill_hi = jnp.concatenate([ends * TM_F, jnp.full((1,), nt * TM_F, jnp.int32)]).astype(jnp.int32)
    pos = _rebase(pos, base.astype(jnp.int32), cap)
    p0, p1 = pos[0], pos[1]
    xs = _dispatch(h, p0, p1, fill_lo, fill_hi, nt * TM_F)
    ys = _ffn(xs, tile_exp.astype(jnp.int32), wg, wu, wd, layer)
    return _combine(h, p0, p1, gates, ys, g, b)


def _retention_tables(c):
    log_g = np.log(1.0 - 2.0 ** (-5.0 - np.arange(N_HEADS, dtype=np.float32))).astype(np.float32)
    idx = np.arange(c, dtype=np.float32)
    diff = idx[:, None] - idx[None, :]
    dmask = np.where(diff >= 0, np.exp(log_g[:, None, None] * np.maximum(diff, 0.0)), 0.0)
    qdec = np.exp(log_g[:, None] * (idx + 1.0))[..., None] * np.ones((1, 1, DK), np.float32)
    kdec = np.exp(log_g[:, None] * (c - 1.0 - idx))[..., None] * np.ones((1, 1, DK), np.float32)
    cdec = np.exp(log_g * c)[:, None, None] * np.ones((1, 1, DV), np.float32)
    return tuple(jnp.asarray(a, F32) for a in (dmask, qdec, kdec, cdec))


def _tri(c, block=None):
    block = c if block is None else block
    return jnp.asarray(np.kron(np.eye(c // block, dtype=np.float32), np.tril(np.ones((block, block), np.float32))),
                       BF16)


def _pad_cols(w, n):
    return jnp.pad(w, ((0, 0), (0, n - w.shape[1])))


def kernel(x, positions, w_in_ab, ret_norm_g, ml_conv_w, ml_conv_b, ml_b_i, ml_b_f, ml_norm_g, w_out_ab,
           w_in_cd, gla_w_alpha, gla_b_alpha, gla_norm_g, ssd_conv_w, ssd_conv_b, ssd_dt_bias, ssd_a_log, ssd_d,
           ssd_norm_g, w_out_cd, w_router, b_router, moe_w_gate, moe_w_up, moe_w_down,
           ln_mix_g, ln_mix_b, ln_ffn_g, ln_ffn_b):
    b, s, d = x.shape
    t = b * s
    c = min(256, s)
    cg = min(64, c)
    h = x.reshape(t, d)
    row = lambda v: v.reshape(1, -1).astype(F32)

    w_pad = _pad_cols(w_router.astype(F32), LANES)
    b_col = b_router.astype(F32).reshape(N_EXPERTS, 1)

    def ffn(hh, layer):
        return _moe_ln(hh, w_pad, b_col, moe_w_gate.astype(F32), moe_w_up.astype(F32), moe_w_down.astype(F32), layer,
                       row(ln_ffn_g[layer]), row(ln_ffn_b[layer]))

    half = np.concatenate([np.arange(0, DK, 2), np.arange(1, DK, 2)])
    perm = np.concatenate([hd * DK + half for hd in range(N_HEADS)])
    w = w_in_ab[0]
    nqk = N_HEADS * DK
    w_main = jnp.concatenate([w[:, :nqk][:, perm], w[:, nqk:2 * nqk][:, perm], w[:, 2 * nqk:P_MAIN]], axis=1)
    inv = ROPE_BASE ** (-jnp.arange(0, DK, 2, dtype=F32) / DK)
    cc, ss = _rope_tables(positions.reshape(t, 1), jnp.concatenate([inv, inv]).reshape(1, DK))
    gate_b = _pad_cols(jnp.concatenate([ml_b_i[0], ml_b_f[0]]).reshape(1, -1).astype(F32), LANES)
    h = _layer_ab(h, w_main.astype(BF16), _pad_cols(w[:, P_MAIN:], LANES).astype(BF16), cc, ss,
                  _retention_tables(c) + (_tri(c),), row(ret_norm_g[0]), row(ml_norm_g[0]),
                  ml_conv_w[0].astype(F32), row(ml_conv_b[0]), gate_b,
                  w_out_ab[0].astype(BF16), row(ln_mix_g[0]), row(ln_mix_b[0]), b, s, c)
    h = ffn(h, 0)

    w = w_in_cd[0]
    g0 = 2 * nqk + 2 * N_HEADS * DV
    s0 = g0 + GLA_RANK
    s1 = s0 + D_MODEL + D_MODEL + 2 * SSD_G * SSD_N
    w_main = jnp.concatenate([w[:, :g0], w[:, s0:s1]], axis=1)
    w_small = _pad_cols(jnp.concatenate([w[:, g0:s0], w[:, s1:]], axis=1), LANES)
    wal = jnp.pad(gla_w_alpha[0], ((0, LANES - GLA_RANK), (0, 0))).astype(BF16)
    lane_pad = lambda v: jnp.pad(v.reshape(1, -1).astype(F32), ((0, 0), (GLA_RANK, LANES - GLA_RANK - SSD_H)))
    dskip = jnp.repeat(ssd_d[0].astype(F32), SSD_P).reshape(1, -1)
    h = _layer_cd(h, w_main.astype(BF16), w_small.astype(BF16), _tri(c), _tri(c, cg), wal, row(gla_b_alpha[0]),
                  row(gla_norm_g[0]), ssd_conv_w[0].astype(F32), row(ssd_conv_b[0]), lane_pad(ssd_dt_bias[0]),
                  lane_pad(ssd_a_log[0]), dskip, row(ssd_norm_g[0]),
                  w_out_cd[0].astype(BF16), row(ln_mix_g[1]), row(ln_mix_b[1]), b, s, c, cg)
    h = ffn(h, 1)
    return h.reshape(b, s, d)
```
